```python
import math
import jax, jax.numpy as jnp
from jax import lax
import numpy as np

D_MODEL = 1024
BATCH = 8
SEQ = 8192
DEPTH = 1
DEC_BATCH = 2
DEC_SEQ = 8192
PAST_LEN = 128

D_MIX = D_MODEL
MLA_HEADS = 4
NOPE_DIM = 128
ROPE_DIM = 64
V_DIM = 128
QK_DIM = NOPE_DIM + ROPE_DIM
Q_LORA = 256
KV_LORA = 128
ROPE_THETA = 10000.0
Q_BLOCK = 128
MLA_WIDTH = MLA_HEADS * V_DIM
FOURIER_WIDTH = D_MIX - MLA_WIDTH
F_GROUPS = 8
F_GROUP_DIM = FOURIER_WIDTH // F_GROUPS
D_IN = Q_LORA + KV_LORA + ROPE_DIM + FOURIER_WIDTH
D_FF = int(math.ceil(8 * D_MODEL / 3 / 256) * 256)
N_MOD = 6
EPS = 1e-6

kernel_name = "hybrid_mla_fnet_adaln_encoder"


def rmsnorm(x, g):
    xf = x.astype(jnp.float32)
    y = xf * lax.rsqrt(jnp.mean(xf * xf, axis=-1, keepdims=True) + EPS)
    return (y * g.astype(jnp.float32)).astype(x.dtype)


def rope_tables(seq):
    pos = jnp.arange(seq, dtype=jnp.float32)
    inv = 1.0 / (ROPE_THETA ** (jnp.arange(0, ROPE_DIM, 2, dtype=jnp.float32) / ROPE_DIM))
    ang = pos[:, None] * inv[None, :]
    return jnp.cos(ang), jnp.sin(ang)


def apply_rope(x, cos, sin):
    half = ROPE_DIM // 2
    xf = x.astype(jnp.float32)
    x1, x2 = xf[..., :half], xf[..., half:]
    out = jnp.concatenate([x1 * cos - x2 * sin, x2 * cos + x1 * sin], axis=-1)
    return out.astype(x.dtype)


def mla_group(q_lat, kv_lat, k_pe, g_q_lat, w_uq, g_kv_lat, w_ukv):
    b, s, _ = q_lat.shape
    cos, sin = rope_tables(s)
    q = (rmsnorm(q_lat, g_q_lat) @ w_uq).reshape(b, s, MLA_HEADS, QK_DIM)
    q_nope, q_pe = q[..., :NOPE_DIM], q[..., NOPE_DIM:]
    q_pe = apply_rope(q_pe, cos[:, None, :], sin[:, None, :])
    kv = (rmsnorm(kv_lat, g_kv_lat) @ w_ukv).reshape(b, s, MLA_HEADS, NOPE_DIM + V_DIM)
    k_nope, v = kv[..., :NOPE_DIM], kv[..., NOPE_DIM:]
    k_pe = apply_rope(k_pe, cos, sin)
    scale = 1.0 / math.sqrt(QK_DIM)
    nb = s // Q_BLOCK
    qn = q_nope.reshape(b, nb, Q_BLOCK, MLA_HEADS, NOPE_DIM).transpose(1, 0, 2, 3, 4)
    qp = q_pe.reshape(b, nb, Q_BLOCK, MLA_HEADS, ROPE_DIM).transpose(1, 0, 2, 3, 4)

    def attend(args):
        qn_b, qp_b = args
        sc = (jnp.einsum('bqhd,bkhd->bhqk', qn_b, k_nope)
              + jnp.einsum('bqhr,bkr->bhqk', qp_b, k_pe)).astype(jnp.float32) * scale
        p = jax.nn.softmax(sc, axis=-1).astype(v.dtype)
        return jnp.einsum('bhqk,bkhd->bqhd', p, v)

    o = lax.map(attend, (qn, qp))
    return o.transpose(1, 0, 2, 3, 4).reshape(b, s, MLA_WIDTH)


def fourier_group(f_in, w_four):
    b, s, _ = f_in.shape
    f = f_in.reshape(b, s, F_GROUPS, F_GROUP_DIM).astype(jnp.float32)
    fr = jnp.real(jnp.fft.fft2(f, axes=(1, 3), norm='ortho')).astype(f_in.dtype)
    out = jnp.einsum('bsgc,gcd->bsgd', fr, w_four)
    return out.reshape(b, s, FOURIER_WIDTH)


def layer(x, c, w_ada, b_ada, g_mix, w_in, g_q_lat, w_uq, g_kv_lat, w_ukv, w_four,
          w_out, g_ffn, w_gate, w_up, w_down):
    mod = jax.nn.silu(c) @ w_ada + b_ada
    sh1, sc1, ga1, sh2, sc2, ga2 = [m[:, None, :] for m in jnp.split(mod, N_MOD, axis=-1)]
    h = rmsnorm(x, g_mix) * (1.0 + sc1) + sh1
    z = h @ w_in
    o1 = Q_LORA
    o2 = o1 + KV_LORA
    o3 = o2 + ROPE_DIM
    q_lat, kv_lat, k_pe, f_in = z[..., :o1], z[..., o1:o2], z[..., o2:o3], z[..., o3:]
    y_mla = mla_group(q_lat, kv_lat, k_pe, g_q_lat, w_uq, g_kv_lat, w_ukv)
    y_four = fourier_group(f_in, w_four)
    mix = jnp.concatenate([y_mla, y_four], axis=-1) @ w_out
    x = x + ga1 * mix
    h2 = rmsnorm(x, g_ffn) * (1.0 + sc2) + sh2
    ffn = (jax.nn.silu(h2 @ w_gate) * (h2 @ w_up)) @ w_down
    return x + ga2 * ffn


def setup_inputs(seed: int = 0) -> dict:
    key = jax.random.key(seed)
    ks = jax.random.split(key, 24)
    f32 = jnp.float32

    def nrm(k, shape, fan_in, mult=1.0):
        return jax.random.normal(k, shape, f32) * (mult * fan_in ** -0.5)

    def gain(k, shape):
        return 1.0 + 0.05 * jax.random.normal(k, shape, f32)

    return {
        "x_prompt": jax.random.normal(ks[0], (BATCH, SEQ, D_MODEL), f32),
        "x_sample": jax.random.normal(ks[1], (DEC_BATCH, DEC_SEQ, D_MODEL), f32),
        "c_prompt": jax.random.normal(ks[2], (BATCH, D_MODEL), f32),
        "c_sample": jax.random.normal(ks[3], (DEC_BATCH, D_MODEL), f32),
        "w_ada": nrm(ks[4], (DEPTH, D_MODEL, N_MOD * D_MODEL), D_MODEL, 0.5),
        "b_ada": 0.02 * jax.random.normal(ks[5], (DEPTH, N_MOD * D_MODEL), f32),
        "g_mix": gain(ks[6], (DEPTH, D_MODEL)),
        "w_in": nrm(ks[7], (DEPTH, D_MODEL, D_IN), D_MODEL),
        "g_q_lat": gain(ks[8], (DEPTH, Q_LORA)),
        "w_uq": nrm(ks[9], (DEPTH, Q_LORA, MLA_HEADS * QK_DIM), Q_LORA),
        "g_kv_lat": gain(ks[10], (DEPTH, KV_LORA)),
        "w_ukv": nrm(ks[11], (DEPTH, KV_LORA, MLA_HEADS * (NOPE_DIM + V_DIM)), KV_LORA),
        "w_four": nrm(ks[12], (DEPTH, F_GROUPS, F_GROUP_DIM, F_GROUP_DIM), F_GROUP_DIM),
        "w_out": nrm(ks[13], (DEPTH, D_MIX, D_MODEL), D_MIX),
        "g_ffn": gain(ks[14], (DEPTH, D_MODEL)),
        "w_gate": nrm(ks[15], (DEPTH, D_MODEL, D_FF), D_MODEL),
        "w_up": nrm(ks[16], (DEPTH, D_MODEL, D_FF), D_MODEL),
        "w_down": nrm(ks[17], (DEPTH, D_FF, D_MODEL), D_FF),
        "g_final": gain(ks[18], (D_MODEL,)),
    }


def reference(x_prompt, x_sample, c_prompt, c_sample, w_ada, b_ada, g_mix, w_in, g_q_lat,
              w_uq, g_kv_lat, w_ukv, w_four, w_out, g_ffn, w_gate, w_up, w_down, g_final):
    def trunk(x, c):
        for l in range(DEPTH):
            x = layer(x, c, w_ada[l], b_ada[l], g_mix[l], w_in[l], g_q_lat[l], w_uq[l],
                      g_kv_lat[l], w_ukv[l], w_four[l], w_out[l], g_ffn[l], w_gate[l],
                      w_up[l], w_down[l])
        return rmsnorm(x, g_final)

    y_prompt = trunk(x_prompt, c_prompt)
    y_sample = trunk(x_sample, c_sample)
    return (y_prompt, y_sample)
```

```python
import functools
import math

import numpy as np
import jax
import jax.numpy as jnp
from jax import lax
from jax.experimental import pallas as pl
from jax.experimental.pallas import tpu as pltpu

F32 = jnp.float32
BF16 = jnp.bfloat16

D_MODEL = 1024
SEQ = 8192
HEADS = 4
NOPE = 128
ROPE = 64
V_DIM = 128
QK_DIM = NOPE + ROPE
Q_LORA = 256
KV_LORA = 128
FOURIER = 512
F_GROUPS = 8
F_GROUP_DIM = 64
D_FF = 2816
N_MOD = 6
EPS = 1e-6
ROPE_THETA = 10000.0
MLA_WIDTH = HEADS * V_DIM

FFT_N1 = 64
FFT_N2 = 128
PITCH_IN = FFT_N2 + 8
PITCH_T = 2 * FFT_N1 + 8
PITCH_OUT = FFT_N1 + 8
LANES = 128

HEAD_PAD = 256
Q_COLS = NOPE + 2 * LANES
Z_COLS = Q_LORA + KV_LORA + 2 * LANES + FOURIER

TOK_TILE = 512
Q_TILE = 256
KV_TILE = 256
VMEM_LIMIT = 56 * 1024 * 1024


def _rms(x, g):
    return x * lax.rsqrt(jnp.mean(x * x, axis=-1, keepdims=True) + EPS) * g


def _mod_kernel(c_ref, w_ref, b_ref, o_ref):
    c = c_ref[...]
    a = (c * jax.nn.sigmoid(c)).astype(BF16)
    o_ref[...] = jnp.dot(a, w_ref[...], preferred_element_type=F32) + b_ref[...]


def _modulation(c_all, w_ada, b_ada):
    rows = c_all.shape[0]
    blk = 1024
    return pl.pallas_call(
        _mod_kernel,
        grid=(N_MOD * D_MODEL // blk,),
        in_specs=[
            pl.BlockSpec((rows, D_MODEL), lambda j: (0, 0)),
            pl.BlockSpec((D_MODEL, blk), lambda j: (0, j)),
            pl.BlockSpec((1, blk), lambda j: (0, j)),
        ],
        out_specs=pl.BlockSpec((rows, blk), lambda j: (0, j)),
        out_shape=jax.ShapeDtypeStruct((rows, N_MOD * D_MODEL), F32),
        name="mod",
    )(c_all, w_ada, b_ada)


def _fold_kernel(c_ref, s_ref, w_ref, o_ref):
    w = w_ref[...]
    mc = jnp.dot(c_ref[...], w, preferred_element_type=F32, precision=lax.Precision.HIGHEST)
    ms = jnp.dot(s_ref[...], w, preferred_element_type=F32, precision=lax.Precision.HIGHEST)
    o_ref[:, :FOURIER] = mc.astype(BF16)
    o_ref[:, FOURIER:] = ms.astype(BF16)


def _fold_fourier(cbd, sbd, wbd):
    return pl.pallas_call(
        _fold_kernel,
        out_shape=jax.ShapeDtypeStruct((FOURIER, 2 * FOURIER), BF16),
        name="fold",
    )(cbd, sbd, wbd)


def _pre_kernel(x_ref, mod_ref, gmix_ref, win_ref, gq_ref, wuq_ref, gkv_ref, wk_ref, wvt_ref,
                mcs_ref, cos_ref, sin_ref, q_ref, k_ref, vt_ref, a_ref, b_ref):
    x = x_ref[0]
    sh1 = mod_ref[0, 0:1, :]
    sc1 = mod_ref[0, 1:2, :]
    h = _rms(x, gmix_ref[...]) * (1.0 + sc1) + sh1
    z = jnp.dot(h.astype(BF16), win_ref[...], preferred_element_type=F32)
    cos = cos_ref[...]
    sin = sin_ref[...]

    o_kv = Q_LORA
    o_kpe = o_kv + KV_LORA
    o_f = o_kpe + 2 * LANES

    qn = _rms(z[:, :Q_LORA], gq_ref[...]).astype(BF16)
    qq = jnp.dot(qn, wuq_ref[...], preferred_element_type=F32)
    scale = 1.0 / math.sqrt(QK_DIM)
    for hd in range(HEADS):
        base = hd * Q_COLS
        rot = (qq[:, base + NOPE:base + NOPE + LANES] * cos
               + qq[:, base + NOPE + LANES:base + Q_COLS] * sin)
        qh = jnp.concatenate([qq[:, base:base + NOPE], rot], axis=-1) * scale
        q_ref[0, hd] = qh.astype(BF16)

    kvn = _rms(z[:, o_kv:o_kpe], gkv_ref[...]).astype(BF16)
    kn = jnp.dot(kvn, wk_ref[...], preferred_element_type=F32)
    krot = z[:, o_kpe:o_kpe + LANES] * cos + z[:, o_kpe + LANES:o_f] * sin
    for hd in range(HEADS):
        kh = jnp.concatenate([kn[:, hd * NOPE:(hd + 1) * NOPE], krot], axis=-1)
        k_ref[0, hd] = kh.astype(BF16)
        vt = lax.dot_general(wvt_ref[hd], kvn, (((1,), (1,)), ((), ())),
                             preferred_element_type=F32)
        for c in range(TOK_TILE // KV_TILE):
            vt_ref[0, hd, c] = vt[:, c * KV_TILE:(c + 1) * KV_TILE].astype(BF16)

    f = z[:, o_f:].astype(BF16)
    ab = jnp.dot(f, mcs_ref[...], preferred_element_type=F32)
    pad = jnp.zeros((PITCH_IN - FFT_N2, FOURIER), F32)
    for j in range(TOK_TILE // FFT_N2):
        r0 = j * PITCH_IN
        a_ref[0, r0:r0 + FFT_N2, :] = ab[j * FFT_N2:(j + 1) * FFT_N2, :FOURIER]
        a_ref[0, r0 + FFT_N2:r0 + PITCH_IN, :] = pad
        b_ref[0, r0:r0 + FFT_N2, :] = ab[j * FFT_N2:(j + 1) * FFT_N2, FOURIER:]
        b_ref[0, r0 + FFT_N2:r0 + PITCH_IN, :] = pad


def _pre(x, mod, gmix, win, gq, wuq, gkv, wk, wvt, mcs, cos_t, sin_t):
    nb = x.shape[0]
    nt = SEQ // TOK_TILE
    rows_in = TOK_TILE // FFT_N2 * PITCH_IN
    const2 = lambda b, t: (0, 0)
    const3 = lambda b, t: (0, 0, 0)
    return pl.pallas_call(
        _pre_kernel,
        grid=(nb, nt),
        in_specs=[
            pl.BlockSpec((1, TOK_TILE, D_MODEL), lambda b, t: (b, t, 0)),
            pl.BlockSpec((1, N_MOD, D_MODEL), lambda b, t: (b, 0, 0)),
            pl.BlockSpec((1, D_MODEL), const2),
            pl.BlockSpec((D_MODEL, Z_COLS), const2),
            pl.BlockSpec((1, Q_LORA), const2),
            pl.BlockSpec((Q_LORA, HEADS * Q_COLS), const2),
            pl.BlockSpec((1, KV_LORA), const2),
            pl.BlockSpec((KV_LORA, HEADS * NOPE), const2),
            pl.BlockSpec((HEADS, V_DIM, KV_LORA), const3),
            pl.BlockSpec((FOURIER, 2 * FOURIER), const2),
            pl.BlockSpec((TOK_TILE, LANES), lambda b, t: (t, 0)),
            pl.BlockSpec((TOK_TILE, LANES), lambda b, t: (t, 0)),
        ],
        out_specs=[
            pl.BlockSpec((1, HEADS, TOK_TILE, HEAD_PAD), lambda b, t: (b, 0, t, 0)),
            pl.BlockSpec((1, HEADS, TOK_TILE, HEAD_PAD), lambda b, t: (b, 0, t, 0)),
            pl.BlockSpec((1, HEADS, TOK_TILE // KV_TILE, V_DIM, KV_TILE), lambda b, t: (b, 0, t, 0, 0)),
            pl.BlockSpec((1, rows_in, FOURIER), lambda b, t: (b, t, 0)),
            pl.BlockSpec((1, rows_in, FOURIER), lambda b, t: (b, t, 0)),
        ],
        out_shape=[
            jax.ShapeDtypeStruct((nb, HEADS, SEQ, HEAD_PAD), BF16),
            jax.ShapeDtypeStruct((nb, HEADS, SEQ, HEAD_PAD), BF16),
            jax.ShapeDtypeStruct((nb, HEADS, SEQ // KV_TILE, V_DIM, KV_TILE), BF16),
            jax.ShapeDtypeStruct((nb, FFT_N1 * PITCH_IN, FOURIER), F32),
            jax.ShapeDtypeStruct((nb, FFT_N1 * PITCH_IN, FOURIER), F32),
        ],
        compiler_params=pltpu.CompilerParams(
            dimension_semantics=("parallel", "parallel"), vmem_limit_bytes=VMEM_LIMIT),
        name="pre",
    )(x, mod, gmix, win, gq, wuq, gkv, wk, wvt, mcs, cos_t, sin_t)


def _attn_kernel(q_ref, k_ref, vt_ref, o_ref):
    q = q_ref[0, 0]

    def body(j, carry):
        m, l, acc = carry
        off = pl.multiple_of(j * KV_TILE, KV_TILE)
        kt = k_ref[0, 0, pl.ds(off, KV_TILE), :]
        st = lax.dot_general(kt, q, (((1,), (1,)), ((), ())),
                             preferred_element_type=F32)
        m_new = jnp.maximum(m, jnp.max(st, axis=0, keepdims=True))
        alpha = jnp.exp(m - m_new)
        p = jnp.exp(st - m_new)
        l_new = alpha * l + jnp.sum(p, axis=0, keepdims=True)
        pv = jnp.dot(vt_ref[0, 0, j], p.astype(BF16), preferred_element_type=F32)
        return m_new, l_new, alpha * acc + pv

    m0 = jnp.full((1, Q_TILE), -jnp.inf, F32)
    l0 = jnp.zeros((1, Q_TILE), F32)
    acc0 = jnp.zeros((V_DIM, Q_TILE), F32)
    _, l, acc = lax.fori_loop(0, SEQ // KV_TILE, body, (m0, l0, acc0))
    o_ref[0] = (acc / l).T.astype(BF16)


def _attention(q, k, vt):
    nb = q.shape[0]
    return pl.pallas_call(
        _attn_kernel,
        grid=(nb, HEADS, SEQ // Q_TILE),
        in_specs=[
            pl.BlockSpec((1, 1, Q_TILE, HEAD_PAD), lambda b, h, i: (b, h, i, 0)),
            pl.BlockSpec((1, 1, SEQ, HEAD_PAD), lambda b, h, i: (b, h, 0, 0)),
            pl.BlockSpec((1, 1, SEQ // KV_TILE, V_DIM, KV_TILE), lambda b, h, i: (b, h, 0, 0, 0)),
        ],
        out_specs=pl.BlockSpec((1, Q_TILE, V_DIM), lambda b, h, i: (b, i, h)),
        out_shape=jax.ShapeDtypeStruct((nb, SEQ, MLA_WIDTH), BF16),
        compiler_params=pltpu.CompilerParams(
            dimension_semantics=("parallel", "parallel", "parallel"), vmem_limit_bytes=VMEM_LIMIT),
        name="attn",
    )(q, k, vt)


def _seqdft_kernel(a_ref, b_ref, g1_ref, f2_ref, y_ref, t_ref):
    def stage1(n2, carry):
        xa = a_ref[0, pl.ds(n2, FFT_N1, stride=PITCH_IN), :]
        xb = b_ref[0, pl.ds(n2, FFT_N1, stride=PITCH_IN), :]
        xab = jnp.concatenate([xa, xb], axis=0).astype(BF16)
        t = jnp.dot(g1_ref[n2], xab, preferred_element_type=F32)
        t_ref[pl.ds(pl.multiple_of(n2 * PITCH_T, 8), 2 * FFT_N1), :] = t
        return carry

    lax.fori_loop(0, FFT_N2, stage1, 0, unroll=4)

    def stage2(k1, carry):
        tr = t_ref[pl.ds(k1, FFT_N2, stride=PITCH_T), :]
        ti = t_ref[pl.ds(FFT_N1 + k1, FFT_N2, stride=PITCH_T), :]
        tt = jnp.concatenate([tr, ti], axis=0).astype(BF16)
        y = jnp.dot(f2_ref[...], tt, preferred_element_type=F32)
        y_ref[0, pl.ds(k1, FFT_N2, stride=PITCH_OUT), :] = y
        return carry

    lax.fori_loop(0, FFT_N1, stage2, 0, unroll=4)

    zero = jnp.zeros((FFT_N2, LANES), F32)
    for j in range(PITCH_OUT - FFT_N1):
        y_ref[0, pl.ds(FFT_N1 + j, FFT_N2, stride=PITCH_OUT), :] = zero


def _seqdft(a, b, g1, f2):
    nb = a.shape[0]
    return pl.pallas_call(
        _seqdft_kernel,
        grid=(nb, FOURIER // LANES),
        in_specs=[
            pl.BlockSpec((1, FFT_N1 * PITCH_IN, LANES), lambda b, c: (b, 0, c)),
            pl.BlockSpec((1, FFT_N1 * PITCH_IN, LANES), lambda b, c: (b, 0, c)),
            pl.BlockSpec((FFT_N2, 2 * FFT_N1, 2 * FFT_N1), lambda b, c: (0, 0, 0)),
            pl.BlockSpec((FFT_N2, 2 * FFT_N2), lambda b, c: (0, 0)),
        ],
        out_specs=pl.BlockSpec((1, FFT_N2 * PITCH_OUT, LANES), lambda b, c: (b, 0, c)),
        out_shape=jax.ShapeDtypeStruct((nb, FFT_N2 * PITCH_OUT, FOURIER), F32),
        scratch_shapes=[pltpu.VMEM((FFT_N2 * PITCH_T, LANES), F32)],
        compiler_params=pltpu.CompilerParams(
            dimension_semantics=("parallel", "parallel"), vmem_limit_bytes=VMEM_LIMIT),
        name="seqdft",
    )(a, b, g1, f2)


FF_CHUNK = D_FF // 2


def _post_kernel(x_ref, mod_ref, ym_ref, yf_ref, woa_ref, wob_ref, gffn_ref, wg_ref, wu_ref, wd_ref,
                 gfin_ref, o_ref):
    x = x_ref[0]
    ga1 = mod_ref[0, 2:3, :]
    sh2 = mod_ref[0, 3:4, :]
    sc2 = mod_ref[0, 4:5, :]
    ga2 = mod_ref[0, 5:6, :]
    yf = jnp.concatenate(
        [yf_ref[0, j * PITCH_OUT:j * PITCH_OUT + FFT_N1, :] for j in range(TOK_TILE // FFT_N1)], axis=0)
    mix = (jnp.dot(ym_ref[0], woa_ref[...], preferred_element_type=F32)
           + jnp.dot(yf.astype(BF16), wob_ref[...], preferred_element_type=F32))
    x1 = x + ga1 * mix
    h2 = (_rms(x1, gffn_ref[...]) * (1.0 + sc2) + sh2).astype(BF16)
    ffn = jnp.zeros((TOK_TILE, D_MODEL), F32)
    for c in range(D_FF // FF_CHUNK):
        lo = c * FF_CHUNK
        g = jnp.dot(h2, wg_ref[:, lo:lo + FF_CHUNK], preferred_element_type=F32)
        u = jnp.dot(h2, wu_ref[:, lo:lo + FF_CHUNK], preferred_element_type=F32)
        act = (g * jax.nn.sigmoid(g) * u).astype(BF16)
        ffn = ffn + jnp.dot(act, wd_ref[lo:lo + FF_CHUNK, :], preferred_element_type=F32)
    x2 = x1 + ga2 * ffn
    o_ref[0] = _rms(x2, gfin_ref[...])


def _post(x, mod, ymla, yfour, woa, wob, gffn, wg, wu, wd, gfin):
    nb = x.shape[0]
    nt = SEQ // TOK_TILE
    rows_out = TOK_TILE // FFT_N1 * PITCH_OUT
    const2 = lambda b, t: (0, 0)
    once = pl.Buffered(1)
    return pl.pallas_call(
        _post_kernel,
        grid=(nb, nt),
        in_specs=[
            pl.BlockSpec((1, TOK_TILE, D_MODEL), lambda b, t: (b, t, 0)),
            pl.BlockSpec((1, N_MOD, D_MODEL), lambda b, t: (b, 0, 0)),
            pl.BlockSpec((1, TOK_TILE, MLA_WIDTH), lambda b, t: (b, t, 0)),
            pl.BlockSpec((1, rows_out, FOURIER), lambda b, t: (b, t, 0)),
            pl.BlockSpec((MLA_WIDTH, D_MODEL), const2, pipeline_mode=once),
            pl.BlockSpec((FOURIER, D_MODEL), const2, pipeline_mode=once),
            pl.BlockSpec((1, D_MODEL), const2),
            pl.BlockSpec((D_MODEL, D_FF), const2, pipeline_mode=once),
            pl.BlockSpec((D_MODEL, D_FF), const2, pipeline_mode=once),
            pl.BlockSpec((D_FF, D_MODEL), const2, pipeline_mode=once),
            pl.BlockSpec((1, D_MODEL), const2),
        ],
        out_specs=pl.BlockSpec((1, TOK_TILE, D_MODEL), lambda b, t: (b, t, 0)),
        out_shape=jax.ShapeDtypeStruct((nb, SEQ, D_MODEL), F32),
        compiler_params=pltpu.CompilerParams(
            dimension_semantics=("parallel", "parallel"), vmem_limit_bytes=VMEM_LIMIT),
        name="post",
    )(x, mod, ymla, yfour, woa, wob, gffn, wg, wu, wd, gfin)


def _dft_constants():
    n = FFT_N1 * FFT_N2
    c = np.arange(F_GROUP_DIM)
    ang = 2.0 * np.pi * np.outer(c, c) / F_GROUP_DIM
    eye = np.eye(F_GROUPS)
    cbd = np.kron(eye, np.cos(ang) / math.sqrt(F_GROUP_DIM))
    sbd = np.kron(eye, np.sin(ang) / math.sqrt(F_GROUP_DIM))
    k1 = np.arange(FFT_N1)[None, :, None]
    n1 = np.arange(FFT_N1)[None, None, :]
    n2 = np.arange(FFT_N2)[:, None, None]
    phi = 2.0 * np.pi * ((k1 * (FFT_N2 * n1 + n2)) % n) / n
    cp, sp = np.cos(phi), np.sin(phi)
    g1 = np.concatenate([np.concatenate([cp, -sp], axis=2),
                         np.concatenate([-sp, -cp], axis=2)], axis=1)
    k2 = np.arange(FFT_N2)[:, None]
    m2 = np.arange(FFT_N2)[None, :]
    th = 2.0 * np.pi * ((k2 * m2) % FFT_N2) / FFT_N2
    f2 = np.concatenate([np.cos(th), np.sin(th)], axis=1) / math.sqrt(n)
    as32 = lambda a: np.asarray(a, np.float32)
    return as32(cbd), as32(sbd), as32(g1), as32(f2)


def _rope_tables():
    pos = jnp.arange(SEQ, dtype=F32)
    inv = 1.0 / (ROPE_THETA ** (jnp.arange(0, ROPE, 2, dtype=F32) / ROPE))
    ang = pos[:, None] * inv[None, :]
    cos, sin = jnp.cos(ang), jnp.sin(ang)
    zeros = jnp.zeros((SEQ, LANES - ROPE), F32)
    return (jnp.concatenate([cos, cos, zeros], axis=1),
            jnp.concatenate([-sin, sin, zeros], axis=1))


def _swap_halves(w):
    half = ROPE // 2
    return jnp.concatenate([w[:, half:], w[:, :half]], axis=1)


def _pad_cols(w, width):
    return jnp.concatenate([w, jnp.zeros((w.shape[0], width - w.shape[1]), w.dtype)], axis=1)


def kernel(x_prompt, x_sample, c_prompt, c_sample, w_ada, b_ada, g_mix, w_in, g_q_lat, w_uq, g_kv_lat,
           w_ukv, w_four, w_out, g_ffn, w_gate, w_up, w_down, g_final):
    l = 0
    cbd, sbd, g1, f2 = _dft_constants()
    g1 = jnp.asarray(g1).astype(BF16)
    f2 = jnp.asarray(f2).astype(BF16)
    cos_t, sin_t = _rope_tables()

    nbp, nbs = x_prompt.shape[0], x_sample.shape[0]
    rows = -(-(nbp + nbs) // 8) * 8
    c_all = jnp.concatenate([c_prompt, c_sample, jnp.zeros((rows - nbp - nbs, D_MODEL), F32)], axis=0)
    mod = _modulation(c_all, w_ada[l].astype(BF16), b_ada[l][None, :])
    mod = mod.reshape(rows, N_MOD, D_MODEL)

    wbd = (jnp.eye(F_GROUPS, dtype=F32)[:, None, :, None] * w_four[l][:, :, None, :]).reshape(FOURIER, FOURIER)
    mcs = _fold_fourier(jnp.asarray(cbd), jnp.asarray(sbd), wbd)

    o1 = Q_LORA + KV_LORA
    o2 = o1 + ROPE
    wi = w_in[l]
    win = jnp.concatenate([wi[:, :o1], _pad_cols(wi[:, o1:o2], LANES),
                           _pad_cols(_swap_halves(wi[:, o1:o2]), LANES), wi[:, o2:]], axis=1).astype(BF16)
    wq = w_uq[l]
    q_parts = []
    for hd in range(HEADS):
        base = hd * QK_DIM
        pe = wq[:, base + NOPE:base + QK_DIM]
        q_parts += [wq[:, base:base + NOPE], _pad_cols(pe, LANES), _pad_cols(_swap_halves(pe), LANES)]
    wuq = jnp.concatenate(q_parts, axis=1).astype(BF16)
    wkv = w_ukv[l].reshape(KV_LORA, HEADS, NOPE + V_DIM)
    wk = wkv[:, :, :NOPE].reshape(KV_LORA, HEADS * NOPE).astype(BF16)
    wvt = jnp.transpose(wkv[:, :, NOPE:], (1, 2, 0)).astype(BF16)
    wo = w_out[l].astype(BF16)
    woa, wob = wo[:MLA_WIDTH], wo[MLA_WIDTH:]
    wg, wu, wd = w_gate[l].astype(BF16), w_up[l].astype(BF16), w_down[l].astype(BF16)
    gmix, gq, gkv = g_mix[l][None, :], g_q_lat[l][None, :], g_kv_lat[l][None, :]
    gffn, gfin = g_ffn[l][None, :], g_final[None, :]

    def trunk(x, mod_x):
        q, k, vt, a, b = _pre(x, mod_x, gmix, win, gq, wuq, gkv, wk, wvt, mcs, cos_t, sin_t)
        ymla = _attention(q, k, vt)
        yfour = _seqdft(a, b, g1, f2)
        return _post(x, mod_x, ymla, yfour, woa, wob, gffn, wg, wu, wd, gfin)

    y_prompt = trunk(x_prompt, mod[:nbp])
    y_sample = trunk(x_sample, mod[nbp:nbp + nbs])
    return (y_prompt, y_sample)
```

```python
import functools
import math

import numpy as np
import jax
import jax.numpy as jnp
from jax import lax
from jax.experimental import pallas as pl
from jax.experimental.pallas import tpu as pltpu

F32 = jnp.float32
BF16 = jnp.bfloat16

D_MODEL = 1024
SEQ = 8192
HEADS = 4
NOPE = 128
ROPE = 64
V_DIM = 128
QK_DIM = NOPE + ROPE
Q_LORA = 256
KV_LORA = 128
FOURIER = 512
F_GROUPS = 8
F_GROUP_DIM = 64
D_FF = 2816
N_MOD = 6
EPS = 1e-6
ROPE_THETA = 10000.0
MLA_WIDTH = HEADS * V_DIM

FFT_N1 = 64
FFT_N2 = 128
PITCH_IN = FFT_N2 + 8
PITCH_T = 2 * FFT_N1 + 8
PITCH_OUT = FFT_N1 + 8
LANES = 128

HEAD_PAD = 256
Q_COLS = NOPE + 2 * LANES
Z_COLS = Q_LORA + KV_LORA + 2 * LANES + FOURIER

TOK_TILE = 512
Q_TILE = 256
KV_TILE = 512
V_ROWS = V_DIM + 16
ATTN_UNROLL = 8
VMEM_LIMIT = 56 * 1024 * 1024


def _rms(x, g):
    return x * lax.rsqrt(jnp.mean(x * x, axis=-1, keepdims=True) + EPS) * g


def _mod_kernel(c_ref, w_ref, b_ref, o_ref):
    c = c_ref[...]
    a = (c * jax.nn.sigmoid(c)).astype(BF16)
    o_ref[...] = jnp.dot(a, w_ref[...], preferred_element_type=F32) + b_ref[...]


def _modulation(c_all, w_ada, b_ada):
    rows = c_all.shape[0]
    blk = 1024
    return pl.pallas_call(
        _mod_kernel,
        grid=(N_MOD * D_MODEL // blk,),
        in_specs=[
            pl.BlockSpec((rows, D_MODEL), lambda j: (0, 0)),
            pl.BlockSpec((D_MODEL, blk), lambda j: (0, j)),
            pl.BlockSpec((1, blk), lambda j: (0, j)),
        ],
        out_specs=pl.BlockSpec((rows, blk), lambda j: (0, j)),
        out_shape=jax.ShapeDtypeStruct((rows, N_MOD * D_MODEL), F32),
        name="mod",
    )(c_all, w_ada, b_ada)


def _fold_kernel(c_ref, s_ref, w_ref, o_ref):
    w = w_ref[...]
    mc = jnp.dot(c_ref[...], w, preferred_element_type=F32, precision=lax.Precision.HIGHEST)
    ms = jnp.dot(s_ref[...], w, preferred_element_type=F32, precision=lax.Precision.HIGHEST)
    o_ref[:, :FOURIER] = mc.astype(BF16)
    o_ref[:, FOURIER:] = ms.astype(BF16)


def _fold_fourier(cbd, sbd, wbd):
    return pl.pallas_call(
        _fold_kernel,
        out_shape=jax.ShapeDtypeStruct((FOURIER, 2 * FOURIER), BF16),
        name="fold",
    )(cbd, sbd, wbd)


def _pre_kernel(x_ref, mod_ref, gmix_ref, win_ref, gq_ref, wuq_ref, gkv_ref, wk_ref, wvt_ref,
                mcs_ref, cos_ref, sin_ref, q_ref, k_ref, vt_ref, a_ref, b_ref):
    x = x_ref[0]
    sh1 = mod_ref[0, 0:1, :]
    sc1 = mod_ref[0, 1:2, :]
    h = _rms(x, gmix_ref[...]) * (1.0 + sc1) + sh1
    z = jnp.dot(h.astype(BF16), win_ref[...], preferred_element_type=F32)
    cos = cos_ref[...]
    sin = sin_ref[...]

    o_kv = Q_LORA
    o_kpe = o_kv + KV_LORA
    o_f = o_kpe + 2 * LANES

    qn = _rms(z[:, :Q_LORA], gq_ref[...]).astype(BF16)
    qq = jnp.dot(qn, wuq_ref[...], preferred_element_type=F32)
    scale = math.log2(math.e) / math.sqrt(QK_DIM)
    for hd in range(HEADS):
        base = hd * Q_COLS
        rot = (qq[:, base + NOPE:base + NOPE + LANES] * cos
               + qq[:, base + NOPE + LANES:base + Q_COLS] * sin)
        qh = jnp.concatenate([qq[:, base:base + NOPE], rot], axis=-1) * scale
        q_ref[0, hd] = qh.astype(BF16)

    kvn = _rms(z[:, o_kv:o_kpe], gkv_ref[...]).astype(BF16)
    kn = jnp.dot(kvn, wk_ref[...], preferred_element_type=F32)
    krot = z[:, o_kpe:o_kpe + LANES] * cos + z[:, o_kpe + LANES:o_f] * sin
    for hd in range(HEADS):
        kh = jnp.concatenate([kn[:, hd * NOPE:(hd + 1) * NOPE], krot], axis=-1)
        k_ref[0, hd] = kh.astype(BF16)
        vt = lax.dot_general(wvt_ref[hd], kvn, (((1,), (1,)), ((), ())),
                             preferred_element_type=F32)
        for c in range(TOK_TILE // KV_TILE):
            vt_ref[0, hd, c, :V_DIM, :] = vt[:, c * KV_TILE:(c + 1) * KV_TILE].astype(BF16)
            vt_ref[0, hd, c, V_DIM:, :] = jnp.ones((V_ROWS - V_DIM, KV_TILE), BF16)

    f = z[:, o_f:].astype(BF16)
    ab = jnp.dot(f, mcs_ref[...], preferred_element_type=F32)
    pad = jnp.zeros((PITCH_IN - FFT_N2, FOURIER), F32)
    for j in range(TOK_TILE // FFT_N2):
        r0 = j * PITCH_IN
        a_ref[0, r0:r0 + FFT_N2, :] = ab[j * FFT_N2:(j + 1) * FFT_N2, :FOURIER]
        a_ref[0, r0 + FFT_N2:r0 + PITCH_IN, :] = pad
        b_ref[0, r0:r0 + FFT_N2, :] = ab[j * FFT_N2:(j + 1) * FFT_N2, FOURIER:]
        b_ref[0, r0 + FFT_N2:r0 + PITCH_IN, :] = pad


def _pre(x, mod, gmix, win, gq, wuq, gkv, wk, wvt, mcs, cos_t, sin_t):
    nb = x.shape[0]
    nt = SEQ // TOK_TILE
    rows_in = TOK_TILE // FFT_N2 * PITCH_IN
    const2 = lambda b, t: (0, 0)
    const3 = lambda b, t: (0, 0, 0)
    return pl.pallas_call(
        _pre_kernel,
        grid=(nb, nt),
        in_specs=[
            pl.BlockSpec((1, TOK_TILE, D_MODEL), lambda b, t: (b, t, 0)),
            pl.BlockSpec((1, N_MOD, D_MODEL), lambda b, t: (b, 0, 0)),
            pl.BlockSpec((1, D_MODEL), const2),
            pl.BlockSpec((D_MODEL, Z_COLS), const2),
            pl.BlockSpec((1, Q_LORA), const2),
            pl.BlockSpec((Q_LORA, HEADS * Q_COLS), const2),
            pl.BlockSpec((1, KV_LORA), const2),
            pl.BlockSpec((KV_LORA, HEADS * NOPE), const2),
            pl.BlockSpec((HEADS, V_DIM, KV_LORA), const3),
            pl.BlockSpec((FOURIER, 2 * FOURIER), const2),
            pl.BlockSpec((TOK_TILE, LANES), lambda b, t: (t, 0)),
            pl.BlockSpec((TOK_TILE, LANES), lambda b, t: (t, 0)),
        ],
        out_specs=[
            pl.BlockSpec((1, HEADS, TOK_TILE, HEAD_PAD), lambda b, t: (b, 0, t, 0)),
            pl.BlockSpec((1, HEADS, TOK_TILE, HEAD_PAD), lambda b, t: (b, 0, t, 0)),
            pl.BlockSpec((1, HEADS, TOK_TILE // KV_TILE, V_ROWS, KV_TILE), lambda b, t: (b, 0, t, 0, 0)),
            pl.BlockSpec((1, rows_in, FOURIER), lambda b, t: (b, t, 0)),
            pl.BlockSpec((1, rows_in, FOURIER), lambda b, t: (b, t, 0)),
        ],
        out_shape=[
            jax.ShapeDtypeStruct((nb, HEADS, SEQ, HEAD_PAD), BF16),
            jax.ShapeDtypeStruct((nb, HEADS, SEQ, HEAD_PAD), BF16),
            jax.ShapeDtypeStruct((nb, HEADS, SEQ // KV_TILE, V_ROWS, KV_TILE), BF16),
            jax.ShapeDtypeStruct((nb, FFT_N1 * PITCH_IN, FOURIER), F32),
            jax.ShapeDtypeStruct((nb, FFT_N1 * PITCH_IN, FOURIER), F32),
        ],
        compiler_params=pltpu.CompilerParams(
            dimension_semantics=("parallel", "parallel"), vmem_limit_bytes=VMEM_LIMIT),
        name="pre",
    )(x, mod, gmix, win, gq, wuq, gkv, wk, wvt, mcs, cos_t, sin_t)


def _attn_kernel(q_ref, k_ref, vt_ref, o_ref, st0_ref, st1_ref, p0_ref, p1_ref):
    q = q_ref[0, 0]
    n_kv = SEQ // KV_TILE

    def scores(c, st_ref):
        off = pl.multiple_of(c * KV_TILE, KV_TILE)
        kt = k_ref[0, 0, pl.ds(off, KV_TILE), :]
        st = lax.dot_general(kt, q, (((1,), (1,)), ((), ())),
                             preferred_element_type=F32)
        st_ref[...] = st
        return jnp.max(st, axis=0, keepdims=True)

    def softmax(st_ref, p_ref, m, mt):
        m_new = jnp.maximum(m, mt)
        p_ref[...] = jnp.exp2(st_ref[...] - m_new).astype(BF16)
        return m_new, jnp.exp2(m - m_new)

    def accumulate(c, p_ref, alpha, acc):
        pv = jnp.dot(vt_ref[0, 0, c], p_ref[...], preferred_element_type=F32)
        return alpha * acc + pv

    def step(c, st_cur, st_nxt, p_cur, p_prev, carry):
        m, mt, alpha_prev, acc = carry
        mt_nxt = scores(jnp.minimum(c + 1, n_kv - 1), st_nxt)
        acc = accumulate(jnp.maximum(c - 1, 0), p_prev, alpha_prev, acc)
        m, alpha = softmax(st_cur, p_cur, m, mt)
        return m, mt_nxt, alpha, acc

    def body(jj, carry):
        c = 2 * jj
        carry = step(c, st0_ref, st1_ref, p0_ref, p1_ref, carry)
        return step(c + 1, st1_ref, st0_ref, p1_ref, p0_ref, carry)

    mt0 = scores(0, st0_ref)
    p1_ref[...] = jnp.zeros((KV_TILE, Q_TILE), BF16)
    init = (jnp.full((1, Q_TILE), -jnp.inf, F32), mt0,
            jnp.ones((1, Q_TILE), F32), jnp.zeros((V_ROWS, Q_TILE), F32))
    _, _, alpha, acc = lax.fori_loop(0, n_kv // 2, body, init, unroll=ATTN_UNROLL)
    acc = accumulate(n_kv - 1, p1_ref, alpha, acc)
    o_ref[0] = (acc[:V_DIM] / acc[V_DIM:V_DIM + 1]).T.astype(BF16)


def _attention(q, k, vt):
    nb = q.shape[0]
    return pl.pallas_call(
        _attn_kernel,
        grid=(nb, HEADS, SEQ // Q_TILE),
        in_specs=[
            pl.BlockSpec((1, 1, Q_TILE, HEAD_PAD), lambda b, h, i: (b, h, i, 0)),
            pl.BlockSpec((1, 1, SEQ, HEAD_PAD), lambda b, h, i: (b, h, 0, 0)),
            pl.BlockSpec((1, 1, SEQ // KV_TILE, V_ROWS, KV_TILE), lambda b, h, i: (b, h, 0, 0, 0)),
        ],
        out_specs=pl.BlockSpec((1, Q_TILE, V_DIM), lambda b, h, i: (b, i, h)),
        out_shape=jax.ShapeDtypeStruct((nb, SEQ, MLA_WIDTH), BF16),
        scratch_shapes=[pltpu.VMEM((KV_TILE, Q_TILE), F32), pltpu.VMEM((KV_TILE, Q_TILE), F32),
                        pltpu.VMEM((KV_TILE, Q_TILE), BF16), pltpu.VMEM((KV_TILE, Q_TILE), BF16)],
        compiler_params=pltpu.CompilerParams(
            dimension_semantics=("parallel", "parallel", "parallel"), vmem_limit_bytes=VMEM_LIMIT),
        name="attn",
    )(q, k, vt)


def _seqdft_kernel(a_ref, b_ref, g1_ref, f2_ref, y_ref, t_ref):
    def stage1(n2, carry):
        xa = a_ref[0, pl.ds(n2, FFT_N1, stride=PITCH_IN), :]
        xb = b_ref[0, pl.ds(n2, FFT_N1, stride=PITCH_IN), :]
        xab = jnp.concatenate([xa, xb], axis=0).astype(BF16)
        t = jnp.dot(g1_ref[n2], xab, preferred_element_type=F32)
        t_ref[pl.ds(pl.multiple_of(n2 * PITCH_T, 8), 2 * FFT_N1), :] = t
        return carry

    lax.fori_loop(0, FFT_N2, stage1, 0, unroll=4)

    def stage2(k1, carry):
        tr = t_ref[pl.ds(k1, FFT_N2, stride=PITCH_T), :]
        ti = t_ref[pl.ds(FFT_N1 + k1, FFT_N2, stride=PITCH_T), :]
        tt = jnp.concatenate([tr, ti], axis=0).astype(BF16)
        y = jnp.dot(f2_ref[...], tt, preferred_element_type=F32)
        y_ref[0, pl.ds(k1, FFT_N2, stride=PITCH_OUT), :] = y
        return carry

    lax.fori_loop(0, FFT_N1, stage2, 0, unroll=4)

    zero = jnp.zeros((FFT_N2, LANES), F32)
    for j in range(PITCH_OUT - FFT_N1):
        y_ref[0, pl.ds(FFT_N1 + j, FFT_N2, stride=PITCH_OUT), :] = zero


def _seqdft(a, b, g1, f2):
    nb = a.shape[0]
    return pl.pallas_call(
        _seqdft_kernel,
        grid=(nb, FOURIER // LANES),
        in_specs=[
            pl.BlockSpec((1, FFT_N1 * PITCH_IN, LANES), lambda b, c: (b, 0, c)),
            pl.BlockSpec((1, FFT_N1 * PITCH_IN, LANES), lambda b, c: (b, 0, c)),
            pl.BlockSpec((FFT_N2, 2 * FFT_N1, 2 * FFT_N1), lambda b, c: (0, 0, 0)),
            pl.BlockSpec((FFT_N2, 2 * FFT_N2), lambda b, c: (0, 0)),
        ],
        out_specs=pl.BlockSpec((1, FFT_N2 * PITCH_OUT, LANES), lambda b, c: (b, 0, c)),
        out_shape=jax.ShapeDtypeStruct((nb, FFT_N2 * PITCH_OUT, FOURIER), F32),
        scratch_shapes=[pltpu.VMEM((FFT_N2 * PITCH_T, LANES), F32)],
        compiler_params=pltpu.CompilerParams(
            dimension_semantics=("parallel", "parallel"), vmem_limit_bytes=VMEM_LIMIT),
        name="seqdft",
    )(a, b, g1, f2)


FF_CHUNK = D_FF // 2


def _post_kernel(x_ref, mod_ref, ym_ref, yf_ref, woa_ref, wob_ref, gffn_ref, wg_ref, wu_ref, wd_ref,
                 gfin_ref, o_ref):
    x = x_ref[0]
    ga1 = mod_ref[0, 2:3, :]
    sh2 = mod_ref[0, 3:4, :]
    sc2 = mod_ref[0, 4:5, :]
    ga2 = mod_ref[0, 5:6, :]
    yf = jnp.concatenate(
        [yf_ref[0, j * PITCH_OUT:j * PITCH_OUT + FFT_N1, :] for j in range(TOK_TILE // FFT_N1)], axis=0)
    mix = (jnp.dot(ym_ref[0], woa_ref[...], preferred_element_type=F32)
           + jnp.dot(yf.astype(BF16), wob_ref[...], preferred_element_type=F32))
    x1 = x + ga1 * mix
    h2 = (_rms(x1, gffn_ref[...]) * (1.0 + sc2) + sh2).astype(BF16)
    ffn = jnp.zeros((TOK_TILE, D_MODEL), F32)
    for c in range(D_FF // FF_CHUNK):
        lo = c * FF_CHUNK
        g = jnp.dot(h2, wg_ref[:, lo:lo + FF_CHUNK], preferred_element_type=F32)
        u = jnp.dot(h2, wu_ref[:, lo:lo + FF_CHUNK], preferred_element_type=F32)
        act = (g * jax.nn.sigmoid(g) * u).astype(BF16)
        ffn = ffn + jnp.dot(act, wd_ref[lo:lo + FF_CHUNK, :], preferred_element_type=F32)
    x2 = x1 + ga2 * ffn
    o_ref[0] = _rms(x2, gfin_ref[...])


def _post(x, mod, ymla, yfour, woa, wob, gffn, wg, wu, wd, gfin):
    nb = x.shape[0]
    nt = SEQ // TOK_TILE
    rows_out = TOK_TILE // FFT_N1 * PITCH_OUT
    const2 = lambda b, t: (0, 0)
    once = pl.Buffered(1)
    return pl.pallas_call(
        _post_kernel,
        grid=(nb, nt),
        in_specs=[
            pl.BlockSpec((1, TOK_TILE, D_MODEL), lambda b, t: (b, t, 0)),
            pl.BlockSpec((1, N_MOD, D_MODEL), lambda b, t: (b, 0, 0)),
            pl.BlockSpec((1, TOK_TILE, MLA_WIDTH), lambda b, t: (b, t, 0)),
            pl.BlockSpec((1, rows_out, FOURIER), lambda b, t: (b, t, 0)),
            pl.BlockSpec((MLA_WIDTH, D_MODEL), const2, pipeline_mode=once),
            pl.BlockSpec((FOURIER, D_MODEL), const2, pipeline_mode=once),
            pl.BlockSpec((1, D_MODEL), const2),
            pl.BlockSpec((D_MODEL, D_FF), const2, pipeline_mode=once),
            pl.BlockSpec((D_MODEL, D_FF), const2, pipeline_mode=once),
            pl.BlockSpec((D_FF, D_MODEL), const2, pipeline_mode=once),
            pl.BlockSpec((1, D_MODEL), const2),
        ],
        out_specs=pl.BlockSpec((1, TOK_TILE, D_MODEL), lambda b, t: (b, t, 0)),
        out_shape=jax.ShapeDtypeStruct((nb, SEQ, D_MODEL), F32),
        compiler_params=pltpu.CompilerParams(
            dimension_semantics=("parallel", "parallel"), vmem_limit_bytes=VMEM_LIMIT),
        name="post",
    )(x, mod, ymla, yfour, woa, wob, gffn, wg, wu, wd, gfin)


def _dft_constants():
    n = FFT_N1 * FFT_N2
    c = np.arange(F_GROUP_DIM)
    ang = 2.0 * np.pi * np.outer(c, c) / F_GROUP_DIM
    eye = np.eye(F_GROUPS)
    cbd = np.kron(eye, np.cos(ang) / math.sqrt(F_GROUP_DIM))
    sbd = np.kron(eye, np.sin(ang) / math.sqrt(F_GROUP_DIM))
    k1 = np.arange(FFT_N1)[None, :, None]
    n1 = np.arange(FFT_N1)[None, None, :]
    n2 = np.arange(FFT_N2)[:, None, None]
    phi = 2.0 * np.pi * ((k1 * (FFT_N2 * n1 + n2)) % n) / n
    cp, sp = np.cos(phi), np.sin(phi)
    g1 = np.concatenate([np.concatenate([cp, -sp], axis=2),
                         np.concatenate([-sp, -cp], axis=2)], axis=1)
    k2 = np.arange(FFT_N2)[:, None]
    m2 = np.arange(FFT_N2)[None, :]
    th = 2.0 * np.pi * ((k2 * m2) % FFT_N2) / FFT_N2
    f2 = np.concatenate([np.cos(th), np.sin(th)], axis=1) / math.sqrt(n)
    as32 = lambda a: np.asarray(a, np.float32)
    return as32(cbd), as32(sbd), as32(g1), as32(f2)


def _rope_tables():
    pos = jnp.arange(SEQ, dtype=F32)
    inv = 1.0 / (ROPE_THETA ** (jnp.arange(0, ROPE, 2, dtype=F32) / ROPE))
    ang = pos[:, None] * inv[None, :]
    cos, sin = jnp.cos(ang), jnp.sin(ang)
    zeros = jnp.zeros((SEQ, LANES - ROPE), F32)
    return (jnp.concatenate([cos, cos, zeros], axis=1),
            jnp.concatenate([-sin, sin, zeros], axis=1))


def _swap_halves(w):
    half = ROPE // 2
    return jnp.concatenate([w[:, half:], w[:, :half]], axis=1)


def _pad_cols(w, width):
    return jnp.concatenate([w, jnp.zeros((w.shape[0], width - w.shape[1]), w.dtype)], axis=1)


def kernel(x_prompt, x_sample, c_prompt, c_sample, w_ada, b_ada, g_mix, w_in, g_q_lat, w_uq, g_kv_lat,
           w_ukv, w_four, w_out, g_ffn, w_gate, w_up, w_down, g_final):
    l = 0
    cbd, sbd, g1, f2 = _dft_constants()
    g1 = jnp.asarray(g1).astype(BF16)
    f2 = jnp.asarray(f2).astype(BF16)
    cos_t, sin_t = _rope_tables()

    nbp, nbs = x_prompt.shape[0], x_sample.shape[0]
    rows = -(-(nbp + nbs) // 8) * 8
    c_all = jnp.concatenate([c_prompt, c_sample, jnp.zeros((rows - nbp - nbs, D_MODEL), F32)], axis=0)
    mod = _modulation(c_all, w_ada[l].astype(BF16), b_ada[l][None, :])
    mod = mod.reshape(rows, N_MOD, D_MODEL)

    wbd = (jnp.eye(F_GROUPS, dtype=F32)[:, None, :, None] * w_four[l][:, :, None, :]).reshape(FOURIER, FOURIER)
    mcs = _fold_fourier(jnp.asarray(cbd), jnp.asarray(sbd), wbd)

    o1 = Q_LORA + KV_LORA
    o2 = o1 + ROPE
    wi = w_in[l]
    win = jnp.concatenate([wi[:, :o1], _pad_cols(wi[:, o1:o2], LANES),
                           _pad_cols(_swap_halves(wi[:, o1:o2]), LANES), wi[:, o2:]], axis=1).astype(BF16)
    wq = w_uq[l]
    q_parts = []
    for hd in range(HEADS):
        base = hd * QK_DIM
        pe = wq[:, base + NOPE:base + QK_DIM]
        q_parts += [wq[:, base:base + NOPE], _pad_cols(pe, LANES), _pad_cols(_swap_halves(pe), LANES)]
    wuq = jnp.concatenate(q_parts, axis=1).astype(BF16)
    wkv = w_ukv[l].reshape(KV_LORA, HEADS, NOPE + V_DIM)
    wk = wkv[:, :, :NOPE].reshape(KV_LORA, HEADS * NOPE).astype(BF16)
    wvt = jnp.transpose(wkv[:, :, NOPE:], (1, 2, 0)).astype(BF16)
    wo = w_out[l].astype(BF16)
    woa, wob = wo[:MLA_WIDTH], wo[MLA_WIDTH:]
    wg, wu, wd = w_gate[l].astype(BF16), w_up[l].astype(BF16), w_down[l].astype(BF16)
    gmix, gq, gkv = g_mix[l][None, :], g_q_lat[l][None, :], g_kv_lat[l][None, :]
    gffn, gfin = g_ffn[l][None, :], g_final[None, :]

    def trunk(x, mod_x):
        q, k, vt, a, b = _pre(x, mod_x, gmix, win, gq, wuq, gkv, wk, wvt, mcs, cos_t, sin_t)
        ymla = _attention(q, k, vt)
        yfour = _seqdft(a, b, g1, f2)
        return _post(x, mod_x, ymla, yfour, woa, wob, gffn, wg, wu, wd, gfin)

    y_prompt = trunk(x_prompt, mod[:nbp])
    y_sample = trunk(x_sample, mod[nbp:nbp + nbs])
    return (y_prompt, y_sample)
```

```python
import functools
import math

import numpy as np
import jax
import jax.numpy as jnp
from jax import lax
from jax.experimental import pallas as pl
from jax.experimental.pallas import tpu as pltpu

F32 = jnp.float32
BF16 = jnp.bfloat16

D_MODEL = 1024
SEQ = 8192
HEADS = 4
NOPE = 128
ROPE = 64
V_DIM = 128
QK_DIM = NOPE + ROPE
Q_LORA = 256
KV_LORA = 128
FOURIER = 512
F_GROUPS = 8
F_GROUP_DIM = 64
D_FF = 2816
N_MOD = 6
EPS = 1e-6
ROPE_THETA = 10000.0
MLA_WIDTH = HEADS * V_DIM

FFT_N1 = 64
FFT_N2 = 128
PITCH_IN = FFT_N2 + 8
PITCH_T = 2 * FFT_N1 + 8
PITCH_OUT = FFT_N1 + 8
LANES = 128
FFT_UNROLL = 8

HEAD_PAD = 256
Q_COLS = NOPE + 2 * LANES
Z_COLS = Q_LORA + KV_LORA + 2 * LANES + FOURIER

TOK_TILE = 512
Q_SUB = 256
Q_SUBTILES = 4
Q_TILE = Q_SUB * Q_SUBTILES
KV_TILE = 512
SCORE_BUFS = 4
PROB_BUFS = 3
V_ROWS = V_DIM + 16
VMEM_LIMIT = 56 * 1024 * 1024


def _rms(x, g):
    return x * lax.rsqrt(jnp.mean(x * x, axis=-1, keepdims=True) + EPS) * g


def _mod_kernel(c_ref, w_ref, b_ref, o_ref):
    c = c_ref[...]
    a = (c * jax.nn.sigmoid(c)).astype(BF16)
    o_ref[...] = jnp.dot(a, w_ref[...], preferred_element_type=F32) + b_ref[...]


def _modulation(c_all, w_ada, b_ada):
    rows = c_all.shape[0]
    blk = 1024
    return pl.pallas_call(
        _mod_kernel,
        grid=(N_MOD * D_MODEL // blk,),
        in_specs=[
            pl.BlockSpec((rows, D_MODEL), lambda j: (0, 0)),
            pl.BlockSpec((D_MODEL, blk), lambda j: (0, j)),
            pl.BlockSpec((1, blk), lambda j: (0, j)),
        ],
        out_specs=pl.BlockSpec((rows, blk), lambda j: (0, j)),
        out_shape=jax.ShapeDtypeStruct((rows, N_MOD * D_MODEL), F32),
        name="mod",
    )(c_all, w_ada, b_ada)


def _fold_kernel(c_ref, s_ref, w_ref, o_ref):
    w = w_ref[...]
    mc = jnp.dot(c_ref[...], w, preferred_element_type=F32, precision=lax.Precision.HIGHEST)
    ms = jnp.dot(s_ref[...], w, preferred_element_type=F32, precision=lax.Precision.HIGHEST)
    o_ref[:, :FOURIER] = mc.astype(BF16)
    o_ref[:, FOURIER:] = ms.astype(BF16)


def _fold_fourier(cbd, sbd, wbd):
    return pl.pallas_call(
        _fold_kernel,
        out_shape=jax.ShapeDtypeStruct((FOURIER, 2 * FOURIER), BF16),
        name="fold",
    )(cbd, sbd, wbd)


def _pre_kernel(x_ref, mod_ref, gmix_ref, win_ref, gq_ref, wuqt_ref, gkv_ref, wk_ref, wvt_ref,
                mcs_ref, cos_ref, sin_ref, cost_ref, sint_ref, qt_ref, k_ref, vt_ref, a_ref, b_ref):
    x = x_ref[0]
    sh1 = mod_ref[0, 0:1, :]
    sc1 = mod_ref[0, 1:2, :]
    h = _rms(x, gmix_ref[...]) * (1.0 + sc1) + sh1
    z = jnp.dot(h.astype(BF16), win_ref[...], preferred_element_type=F32)
    cos = cos_ref[...]
    sin = sin_ref[...]

    o_kv = Q_LORA
    o_kpe = o_kv + KV_LORA
    o_f = o_kpe + 2 * LANES

    qn = _rms(z[:, :Q_LORA], gq_ref[...]).astype(BF16)
    qqt = lax.dot_general(wuqt_ref[...], qn, (((1,), (1,)), ((), ())),
                          preferred_element_type=F32)
    cos_tr = cost_ref[...]
    sin_tr = sint_ref[...]
    scale = math.log2(math.e) / math.sqrt(QK_DIM)
    for hd in range(HEADS):
        base = hd * Q_COLS
        rot = (qqt[base + NOPE:base + NOPE + LANES, :] * cos_tr
               + qqt[base + NOPE + LANES:base + Q_COLS, :] * sin_tr)
        qt_ref[0, hd, :NOPE, :] = (qqt[base:base + NOPE, :] * scale).astype(BF16)
        qt_ref[0, hd, NOPE:, :] = (rot * scale).astype(BF16)

    kvn = _rms(z[:, o_kv:o_kpe], gkv_ref[...]).astype(BF16)
    kn = jnp.dot(kvn, wk_ref[...], preferred_element_type=F32)
    krot = z[:, o_kpe:o_kpe + LANES] * cos + z[:, o_kpe + LANES:o_f] * sin
    for hd in range(HEADS):
        kh = jnp.concatenate([kn[:, hd * NOPE:(hd + 1) * NOPE], krot], axis=-1)
        k_ref[0, hd] = kh.astype(BF16)
        vt = lax.dot_general(wvt_ref[hd], kvn, (((1,), (1,)), ((), ())),
                             preferred_element_type=F32)
        for c in range(TOK_TILE // KV_TILE):
            vt_ref[0, hd, c, :V_DIM, :] = vt[:, c * KV_TILE:(c + 1) * KV_TILE].astype(BF16)
            vt_ref[0, hd, c, V_DIM:, :] = jnp.ones((V_ROWS - V_DIM, KV_TILE), BF16)

    f = z[:, o_f:].astype(BF16)
    ab = jnp.dot(f, mcs_ref[...], preferred_element_type=F32)
    pad = jnp.zeros((PITCH_IN - FFT_N2, FOURIER), F32)
    for j in range(TOK_TILE // FFT_N2):
        r0 = j * PITCH_IN
        a_ref[0, r0:r0 + FFT_N2, :] = ab[j * FFT_N2:(j + 1) * FFT_N2, :FOURIER]
        a_ref[0, r0 + FFT_N2:r0 + PITCH_IN, :] = pad
        b_ref[0, r0:r0 + FFT_N2, :] = ab[j * FFT_N2:(j + 1) * FFT_N2, FOURIER:]
        b_ref[0, r0 + FFT_N2:r0 + PITCH_IN, :] = pad


def _pre(x, mod, gmix, win, gq, wuqt, gkv, wk, wvt, mcs, cos_t, sin_t, cos_tr, sin_tr):
    nb = x.shape[0]
    nt = SEQ // TOK_TILE
    rows_in = TOK_TILE // FFT_N2 * PITCH_IN
    const2 = lambda b, t: (0, 0)
    const3 = lambda b, t: (0, 0, 0)
    return pl.pallas_call(
        _pre_kernel,
        grid=(nb, nt),
        in_specs=[
            pl.BlockSpec((1, TOK_TILE, D_MODEL), lambda b, t: (b, t, 0)),
            pl.BlockSpec((1, N_MOD, D_MODEL), lambda b, t: (b, 0, 0)),
            pl.BlockSpec((1, D_MODEL), const2),
            pl.BlockSpec((D_MODEL, Z_COLS), const2),
            pl.BlockSpec((1, Q_LORA), const2),
            pl.BlockSpec((HEADS * Q_COLS, Q_LORA), const2),
            pl.BlockSpec((1, KV_LORA), const2),
            pl.BlockSpec((KV_LORA, HEADS * NOPE), const2),
            pl.BlockSpec((HEADS, V_DIM, KV_LORA), const3),
            pl.BlockSpec((FOURIER, 2 * FOURIER), const2),
            pl.BlockSpec((TOK_TILE, LANES), lambda b, t: (t, 0)),
            pl.BlockSpec((TOK_TILE, LANES), lambda b, t: (t, 0)),
            pl.BlockSpec((LANES, TOK_TILE), lambda b, t: (0, t)),
            pl.BlockSpec((LANES, TOK_TILE), lambda b, t: (0, t)),
        ],
        out_specs=[
            pl.BlockSpec((1, HEADS, HEAD_PAD, TOK_TILE), lambda b, t: (b, 0, 0, t)),
            pl.BlockSpec((1, HEADS, TOK_TILE, HEAD_PAD), lambda b, t: (b, 0, t, 0)),
            pl.BlockSpec((1, HEADS, TOK_TILE // KV_TILE, V_ROWS, KV_TILE), lambda b, t: (b, 0, t, 0, 0)),
            pl.BlockSpec((1, rows_in, FOURIER), lambda b, t: (b, t, 0)),
            pl.BlockSpec((1, rows_in, FOURIER), lambda b, t: (b, t, 0)),
        ],
        out_shape=[
            jax.ShapeDtypeStruct((nb, HEADS, HEAD_PAD, SEQ), BF16),
            jax.ShapeDtypeStruct((nb, HEADS, SEQ, HEAD_PAD), BF16),
            jax.ShapeDtypeStruct((nb, HEADS, SEQ // KV_TILE, V_ROWS, KV_TILE), BF16),
            jax.ShapeDtypeStruct((nb, FFT_N1 * PITCH_IN, FOURIER), F32),
            jax.ShapeDtypeStruct((nb, FFT_N1 * PITCH_IN, FOURIER), F32),
        ],
        compiler_params=pltpu.CompilerParams(
            dimension_semantics=("parallel", "parallel"), vmem_limit_bytes=VMEM_LIMIT),
        name="pre",
    )(x, mod, gmix, win, gq, wuqt, gkv, wk, wvt, mcs, cos_t, sin_t, cos_tr, sin_tr)


def _attn_kernel(qt_ref, k_ref, vt_ref, o_ref, *bufs):
    st_refs, p_refs = bufs[:SCORE_BUFS], bufs[SCORE_BUFS:]
    n_kv = SEQ // KV_TILE
    n_items = Q_SUBTILES * n_kv

    def scores(i):
        s, c = divmod(i, n_kv)
        kt = k_ref[0, 0, c * KV_TILE:(c + 1) * KV_TILE, :]
        qt = qt_ref[0, 0, :, s * Q_SUB:(s + 1) * Q_SUB]
        st = jnp.dot(kt, qt, preferred_element_type=F32)
        st_refs[i % SCORE_BUFS][...] = st
        return jnp.max(st, axis=0, keepdims=True)

    def value_matmul(i):
        return jnp.dot(vt_ref[0, 0, i % n_kv], p_refs[i % PROB_BUFS][...],
                       preferred_element_type=F32)

    chunk_max = [scores(i) for i in range(SCORE_BUFS - 1)]
    for i in range(n_items):
        s, c = divmod(i, n_kv)
        if i + SCORE_BUFS - 1 < n_items:
            chunk_max.append(scores(i + SCORE_BUFS - 1))
        m_new = chunk_max[i] if c == 0 else jnp.maximum(m, chunk_max[i])
        p_refs[i % PROB_BUFS][...] = jnp.exp2(st_refs[i % SCORE_BUFS][...] - m_new).astype(BF16)
        pv = value_matmul(i)
        acc = pv if c == 0 else jnp.exp2(m - m_new) * acc + pv
        m = m_new
        if c == n_kv - 1:
            o_ref[0, s * Q_SUB:(s + 1) * Q_SUB, :] = (acc[:V_DIM] / acc[V_DIM:V_DIM + 1]).T.astype(BF16)


def _attention(qt, k, vt):
    nb = qt.shape[0]
    return pl.pallas_call(
        _attn_kernel,
        grid=(nb, HEADS, SEQ // Q_TILE),
        in_specs=[
            pl.BlockSpec((1, 1, HEAD_PAD, Q_TILE), lambda b, h, i: (b, h, 0, i)),
            pl.BlockSpec((1, 1, SEQ, HEAD_PAD), lambda b, h, i: (b, h, 0, 0)),
            pl.BlockSpec((1, 1, SEQ // KV_TILE, V_ROWS, KV_TILE), lambda b, h, i: (b, h, 0, 0, 0)),
        ],
        out_specs=pl.BlockSpec((1, Q_TILE, V_DIM), lambda b, h, i: (b, i, h)),
        out_shape=jax.ShapeDtypeStruct((nb, SEQ, MLA_WIDTH), BF16),
        scratch_shapes=([pltpu.VMEM((KV_TILE, Q_SUB), F32)] * SCORE_BUFS
                        + [pltpu.VMEM((KV_TILE, Q_SUB), BF16)] * PROB_BUFS),
        compiler_params=pltpu.CompilerParams(
            dimension_semantics=("parallel", "parallel", "parallel"), vmem_limit_bytes=VMEM_LIMIT),
        name="attn",
    )(qt, k, vt)


def _seqdft_kernel(a_ref, b_ref, g1_ref, f2_ref, y_ref, t_ref):
    def stage1(n2, carry):
        xa = a_ref[0, pl.ds(n2, FFT_N1, stride=PITCH_IN), :]
        xb = b_ref[0, pl.ds(n2, FFT_N1, stride=PITCH_IN), :]
        xab = jnp.concatenate([xa, xb], axis=0).astype(BF16)
        t = jnp.dot(g1_ref[n2], xab, preferred_element_type=F32)
        t_ref[pl.ds(pl.multiple_of(n2 * PITCH_T, 8), 2 * FFT_N1), :] = t
        return carry

    lax.fori_loop(0, FFT_N2, stage1, 0, unroll=FFT_UNROLL)

    def stage2(k1, carry):
        tr = t_ref[pl.ds(k1, FFT_N2, stride=PITCH_T), :]
        ti = t_ref[pl.ds(FFT_N1 + k1, FFT_N2, stride=PITCH_T), :]
        tt = jnp.concatenate([tr, ti], axis=0).astype(BF16)
        y = jnp.dot(f2_ref[...], tt, preferred_element_type=F32)
        y_ref[0, pl.ds(k1, FFT_N2, stride=PITCH_OUT), :] = y
        return carry

    lax.fori_loop(0, FFT_N1, stage2, 0, unroll=FFT_UNROLL)

    zero = jnp.zeros((FFT_N2, LANES), F32)
    for j in range(PITCH_OUT - FFT_N1):
        y_ref[0, pl.ds(FFT_N1 + j, FFT_N2, stride=PITCH_OUT), :] = zero


def _seqdft(a, b, g1, f2):
    nb = a.shape[0]
    return pl.pallas_call(
        _seqdft_kernel,
        grid=(nb, FOURIER // LANES),
        in_specs=[
            pl.BlockSpec((1, FFT_N1 * PITCH_IN, LANES), lambda b, c: (b, 0, c)),
            pl.BlockSpec((1, FFT_N1 * PITCH_IN, LANES), lambda b, c: (b, 0, c)),
            pl.BlockSpec((FFT_N2, 2 * FFT_N1, 2 * FFT_N1), lambda b, c: (0, 0, 0)),
            pl.BlockSpec((FFT_N2, 2 * FFT_N2), lambda b, c: (0, 0)),
        ],
        out_specs=pl.BlockSpec((1, FFT_N2 * PITCH_OUT, LANES), lambda b, c: (b, 0, c)),
        out_shape=jax.ShapeDtypeStruct((nb, FFT_N2 * PITCH_OUT, FOURIER), F32),
        scratch_shapes=[pltpu.VMEM((FFT_N2 * PITCH_T, LANES), F32)],
        compiler_params=pltpu.CompilerParams(
            dimension_semantics=("parallel", "parallel"), vmem_limit_bytes=VMEM_LIMIT),
        name="seqdft",
    )(a, b, g1, f2)


FF_CHUNK = D_FF // 2


def _post_kernel(x_ref, mod_ref, ym_ref, yf_ref, woa_ref, wob_ref, gffn_ref, wg_ref, wu_ref, wd_ref,
                 gfin_ref, o_ref):
    x = x_ref[0]
    ga1 = mod_ref[0, 2:3, :]
    sh2 = mod_ref[0, 3:4, :]
    sc2 = mod_ref[0, 4:5, :]
    ga2 = mod_ref[0, 5:6, :]
    yf = jnp.concatenate(
        [yf_ref[0, j * PITCH_OUT:j * PITCH_OUT + FFT_N1, :] for j in range(TOK_TILE // FFT_N1)], axis=0)
    mix = (jnp.dot(ym_ref[0], woa_ref[...], preferred_element_type=F32)
           + jnp.dot(yf.astype(BF16), wob_ref[...], preferred_element_type=F32))
    x1 = x + ga1 * mix
    h2 = (_rms(x1, gffn_ref[...]) * (1.0 + sc2) + sh2).astype(BF16)
    ffn = jnp.zeros((TOK_TILE, D_MODEL), F32)
    for c in range(D_FF // FF_CHUNK):
        lo = c * FF_CHUNK
        g = jnp.dot(h2, wg_ref[:, lo:lo + FF_CHUNK], preferred_element_type=F32)
        u = jnp.dot(h2, wu_ref[:, lo:lo + FF_CHUNK], preferred_element_type=F32)
        act = (g * jax.nn.sigmoid(g) * u).astype(BF16)
        ffn = ffn + jnp.dot(act, wd_ref[lo:lo + FF_CHUNK, :], preferred_element_type=F32)
    x2 = x1 + ga2 * ffn
    o_ref[0] = _rms(x2, gfin_ref[...])


def _post(x, mod, ymla, yfour, woa, wob, gffn, wg, wu, wd, gfin):
    nb = x.shape[0]
    nt = SEQ // TOK_TILE
    rows_out = TOK_TILE // FFT_N1 * PITCH_OUT
    const2 = lambda b, t: (0, 0)
    once = pl.Buffered(1)
    return pl.pallas_call(
        _post_kernel,
        grid=(nb, nt),
        in_specs=[
            pl.BlockSpec((1, TOK_TILE, D_MODEL), lambda b, t: (b, t, 0)),
            pl.BlockSpec((1, N_MOD, D_MODEL), lambda b, t: (b, 0, 0)),
            pl.BlockSpec((1, TOK_TILE, MLA_WIDTH), lambda b, t: (b, t, 0)),
            pl.BlockSpec((1, rows_out, FOURIER), lambda b, t: (b, t, 0)),
            pl.BlockSpec((MLA_WIDTH, D_MODEL), const2, pipeline_mode=once),
            pl.BlockSpec((FOURIER, D_MODEL), const2, pipeline_mode=once),
            pl.BlockSpec((1, D_MODEL), const2),
            pl.BlockSpec((D_MODEL, D_FF), const2, pipeline_mode=once),
            pl.BlockSpec((D_MODEL, D_FF), const2, pipeline_mode=once),
            pl.BlockSpec((D_FF, D_MODEL), const2, pipeline_mode=once),
            pl.BlockSpec((1, D_MODEL), const2),
        ],
        out_specs=pl.BlockSpec((1, TOK_TILE, D_MODEL), lambda b, t: (b, t, 0)),
        out_shape=jax.ShapeDtypeStruct((nb, SEQ, D_MODEL), F32),
        compiler_params=pltpu.CompilerParams(
            dimension_semantics=("parallel", "parallel"), vmem_limit_bytes=VMEM_LIMIT),
        name="post",
    )(x, mod, ymla, yfour, woa, wob, gffn, wg, wu, wd, gfin)


def _dft_constants():
    n = FFT_N1 * FFT_N2
    c = np.arange(F_GROUP_DIM)
    ang = 2.0 * np.pi * np.outer(c, c) / F_GROUP_DIM
    eye = np.eye(F_GROUPS)
    cbd = np.kron(eye, np.cos(ang) / math.sqrt(F_GROUP_DIM))
    sbd = np.kron(eye, np.sin(ang) / math.sqrt(F_GROUP_DIM))
    k1 = np.arange(FFT_N1)[None, :, None]
    n1 = np.arange(FFT_N1)[None, None, :]
    n2 = np.arange(FFT_N2)[:, None, None]
    phi = 2.0 * np.pi * ((k1 * (FFT_N2 * n1 + n2)) % n) / n
    cp, sp = np.cos(phi), np.sin(phi)
    g1 = np.concatenate([np.concatenate([cp, -sp], axis=2),
                         np.concatenate([-sp, -cp], axis=2)], axis=1)
    k2 = np.arange(FFT_N2)[:, None]
    m2 = np.arange(FFT_N2)[None, :]
    th = 2.0 * np.pi * ((k2 * m2) % FFT_N2) / FFT_N2
    f2 = np.concatenate([np.cos(th), np.sin(th)], axis=1) / math.sqrt(n)
    as32 = lambda a: np.asarray(a, np.float32)
    return as32(cbd), as32(sbd), as32(g1), as32(f2)


def _rope_tables():
    pos = jnp.arange(SEQ, dtype=F32)
    inv = 1.0 / (ROPE_THETA ** (jnp.arange(0, ROPE, 2, dtype=F32) / ROPE))
    ang = pos[:, None] * inv[None, :]
    cos, sin = jnp.cos(ang), jnp.sin(ang)
    zeros = jnp.zeros((SEQ, LANES - ROPE), F32)
    return (jnp.concatenate([cos, cos, zeros], axis=1),
            jnp.concatenate([-sin, sin, zeros], axis=1))


def _swap_halves(w):
    half = ROPE // 2
    return jnp.concatenate([w[:, half:], w[:, :half]], axis=1)


def _pad_cols(w, width):
    return jnp.concatenate([w, jnp.zeros((w.shape[0], width - w.shape[1]), w.dtype)], axis=1)


def kernel(x_prompt, x_sample, c_prompt, c_sample, w_ada, b_ada, g_mix, w_in, g_q_lat, w_uq, g_kv_lat,
           w_ukv, w_four, w_out, g_ffn, w_gate, w_up, w_down, g_final):
    l = 0
    cbd, sbd, g1, f2 = _dft_constants()
    g1 = jnp.asarray(g1).astype(BF16)
    f2 = jnp.asarray(f2).astype(BF16)
    cos_t, sin_t = _rope_tables()
    cos_tr, sin_tr = cos_t.T, sin_t.T

    nbp, nbs = x_prompt.shape[0], x_sample.shape[0]
    rows = -(-(nbp + nbs) // 8) * 8
    c_all = jnp.concatenate([c_prompt, c_sample, jnp.zeros((rows - nbp - nbs, D_MODEL), F32)], axis=0)
    mod = _modulation(c_all, w_ada[l].astype(BF16), b_ada[l][None, :])
    mod = mod.reshape(rows, N_MOD, D_MODEL)

    wbd = (jnp.eye(F_GROUPS, dtype=F32)[:, None, :, None] * w_four[l][:, :, None, :]).reshape(FOURIER, FOURIER)
    mcs = _fold_fourier(jnp.asarray(cbd), jnp.asarray(sbd), wbd)

    o1 = Q_LORA + KV_LORA
    o2 = o1 + ROPE
    wi = w_in[l]
    win = jnp.concatenate([wi[:, :o1], _pad_cols(wi[:, o1:o2], LANES),
                           _pad_cols(_swap_halves(wi[:, o1:o2]), LANES), wi[:, o2:]], axis=1).astype(BF16)
    wq = w_uq[l]
    q_parts = []
    for hd in range(HEADS):
        base = hd * QK_DIM
        pe = wq[:, base + NOPE:base + QK_DIM]
        q_parts += [wq[:, base:base + NOPE], _pad_cols(pe, LANES), _pad_cols(_swap_halves(pe), LANES)]
    wuqt = jnp.concatenate(q_parts, axis=1).T.astype(BF16)
    wkv = w_ukv[l].reshape(KV_LORA, HEADS, NOPE + V_DIM)
    wk = wkv[:, :, :NOPE].reshape(KV_LORA, HEADS * NOPE).astype(BF16)
    wvt = jnp.transpose(wkv[:, :, NOPE:], (1, 2, 0)).astype(BF16)
    wo = w_out[l].astype(BF16)
    woa, wob = wo[:MLA_WIDTH], wo[MLA_WIDTH:]
    wg, wu, wd = w_gate[l].astype(BF16), w_up[l].astype(BF16), w_down[l].astype(BF16)
    gmix, gq, gkv = g_mix[l][None, :], g_q_lat[l][None, :], g_kv_lat[l][None, :]
    gffn, gfin = g_ffn[l][None, :], g_final[None, :]

    def trunk(x, mod_x):
        qt, k, vt, a, b = _pre(x, mod_x, gmix, win, gq, wuqt, gkv, wk, wvt, mcs, cos_t, sin_t, cos_tr, sin_tr)
        ymla = _attention(qt, k, vt)
        yfour = _seqdft(a, b, g1, f2)
        return _post(x, mod_x, ymla, yfour, woa, wob, gffn, wg, wu, wd, gfin)

    y_prompt = trunk(x_prompt, mod[:nbp])
    y_sample = trunk(x_sample, mod[nbp:nbp + nbs])
    return (y_prompt, y_sample)
```

```python
import functools
import math

import numpy as np
import jax
import jax.numpy as jnp
from jax import lax
from jax.experimental import pallas as pl
from jax.experimental.pallas import tpu as pltpu

F32 = jnp.float32
BF16 = jnp.bfloat16

D_MODEL = 1024
SEQ = 8192
HEADS = 4
NOPE = 128
ROPE = 64
V_DIM = 128
QK_DIM = NOPE + ROPE
Q_LORA = 256
KV_LORA = 128
FOURIER = 512
F_GROUPS = 8
F_GROUP_DIM = 64
D_FF = 2816
N_MOD = 6
EPS = 1e-6
ROPE_THETA = 10000.0
MLA_WIDTH = HEADS * V_DIM

FFT_N1 = 64
FFT_N2 = 128
PITCH_IN = FFT_N2 + 8
PITCH_T = 2 * FFT_N1 + 8
PITCH_OUT = FFT_N1 + 8
LANES = 128
FFT_UNROLL = 16

HEAD_PAD = 256
Q_COLS = NOPE + 2 * ROPE
Z_COLS = Q_LORA + KV_LORA + 2 * ROPE + FOURIER

TOK_TILE = 512
POST_TILE = 512
Q_SUB = 256
Q_SUBTILES = 4
Q_TILE = Q_SUB * Q_SUBTILES
KV_TILE = 512
SCORE_BUFS = 4
PROB_BUFS = 3
V_ROWS = V_DIM + 16
VMEM_LIMIT = 56 * 1024 * 1024


def _rms(x, g):
    return x * lax.rsqrt(jnp.mean(x * x, axis=-1, keepdims=True) + EPS) * g


def _mod_kernel(c_ref, w_ref, b_ref, o_ref):
    c = c_ref[...]
    a = (c * jax.nn.sigmoid(c)).astype(BF16)
    o_ref[...] = jnp.dot(a, w_ref[...], preferred_element_type=F32) + b_ref[...]


def _modulation(c_all, w_ada, b_ada):
    rows = c_all.shape[0]
    blk = 1024
    return pl.pallas_call(
        _mod_kernel,
        grid=(N_MOD * D_MODEL // blk,),
        in_specs=[
            pl.BlockSpec((rows, D_MODEL), lambda j: (0, 0)),
            pl.BlockSpec((D_MODEL, blk), lambda j: (0, j)),
            pl.BlockSpec((1, blk), lambda j: (0, j)),
        ],
        out_specs=pl.BlockSpec((rows, blk), lambda j: (0, j)),
        out_shape=jax.ShapeDtypeStruct((rows, N_MOD * D_MODEL), F32),
        name="mod",
    )(c_all, w_ada, b_ada)


def _fold_kernel(c_ref, s_ref, w_ref, o_ref):
    w = w_ref[...]
    mc = jnp.dot(c_ref[...], w, preferred_element_type=F32, precision=lax.Precision.HIGHEST)
    ms = jnp.dot(s_ref[...], w, preferred_element_type=F32, precision=lax.Precision.HIGHEST)
    o_ref[:, :FOURIER] = mc.astype(BF16)
    o_ref[:, FOURIER:] = ms.astype(BF16)


def _fold_fourier(cbd, sbd, wbd):
    return pl.pallas_call(
        _fold_kernel,
        out_shape=jax.ShapeDtypeStruct((FOURIER, 2 * FOURIER), BF16),
        name="fold",
    )(cbd, sbd, wbd)


def _pre_kernel(x_ref, mod_ref, gmix_ref, win_ref, gq_ref, wuqt_ref, gkv_ref, wk_ref, wvt_ref,
                mcs_ref, cs_ref, cost_ref, sint_ref, qt_ref, k_ref, vt_ref, a_ref, b_ref):
    x = x_ref[0]
    sh1 = mod_ref[0, 0:1, :]
    sc1 = mod_ref[0, 1:2, :]
    h = _rms(x, gmix_ref[...]) * (1.0 + sc1) + sh1
    z = jnp.dot(h.astype(BF16), win_ref[...], preferred_element_type=F32)

    o_kv = Q_LORA
    o_kpe = o_kv + KV_LORA
    o_f = o_kpe + 2 * ROPE

    qn = _rms(z[:, :Q_LORA], gq_ref[...]).astype(BF16)
    qqt = lax.dot_general(wuqt_ref[...], qn, (((1,), (1,)), ((), ())),
                          preferred_element_type=F32)
    cos_tr = cost_ref[...]
    sin_tr = sint_ref[...]
    scale = math.log2(math.e) / math.sqrt(QK_DIM)
    for hd in range(HEADS):
        base = hd * Q_COLS
        rot = (qqt[base + NOPE:base + QK_DIM, :] * cos_tr
               + qqt[base + QK_DIM:base + Q_COLS, :] * sin_tr)
        qt_ref[0, hd, :NOPE, :] = (qqt[base:base + NOPE, :] * scale).astype(BF16)
        qt_ref[0, hd, NOPE:QK_DIM, :] = (rot * scale).astype(BF16)
        qt_ref[0, hd, QK_DIM:, :] = jnp.zeros((HEAD_PAD - QK_DIM, TOK_TILE), BF16)

    kvn = _rms(z[:, o_kv:o_kpe], gkv_ref[...]).astype(BF16)
    kn = jnp.dot(kvn, wk_ref[...], preferred_element_type=F32)
    t = z[:, o_kpe:o_f] * cs_ref[...]
    lane = lax.broadcasted_iota(jnp.int32, t.shape, 1)
    krot = jnp.where(lane < ROPE, t + pltpu.roll(t, ROPE, axis=1), 0.0)
    for hd in range(HEADS):
        kh = jnp.concatenate([kn[:, hd * NOPE:(hd + 1) * NOPE], krot], axis=-1)
        k_ref[0, hd] = kh.astype(BF16)
        vt = lax.dot_general(wvt_ref[hd], kvn, (((1,), (1,)), ((), ())),
                             preferred_element_type=F32)
        for c in range(TOK_TILE // KV_TILE):
            vt_ref[0, hd, c, :V_DIM, :] = vt[:, c * KV_TILE:(c + 1) * KV_TILE].astype(BF16)
            vt_ref[0, hd, c, V_DIM:, :] = jnp.ones((V_ROWS - V_DIM, KV_TILE), BF16)

    f = z[:, o_f:].astype(BF16)
    ab = jnp.dot(f, mcs_ref[...], preferred_element_type=F32)
    pad = jnp.zeros((PITCH_IN - FFT_N2, FOURIER), F32)
    for j in range(TOK_TILE // FFT_N2):
        r0 = j * PITCH_IN
        a_ref[0, r0:r0 + FFT_N2, :] = ab[j * FFT_N2:(j + 1) * FFT_N2, :FOURIER]
        a_ref[0, r0 + FFT_N2:r0 + PITCH_IN, :] = pad
        b_ref[0, r0:r0 + FFT_N2, :] = ab[j * FFT_N2:(j + 1) * FFT_N2, FOURIER:]
        b_ref[0, r0 + FFT_N2:r0 + PITCH_IN, :] = pad


def _pre(x, mod, gmix, win, gq, wuqt, gkv, wk, wvt, mcs, cs_t, cos_tr, sin_tr):
    nb = x.shape[0]
    nt = SEQ // TOK_TILE
    rows_in = TOK_TILE // FFT_N2 * PITCH_IN
    const2 = lambda b, t: (0, 0)
    const3 = lambda b, t: (0, 0, 0)
    return pl.pallas_call(
        _pre_kernel,
        grid=(nb, nt),
        in_specs=[
            pl.BlockSpec((1, TOK_TILE, D_MODEL), lambda b, t: (b, t, 0)),
            pl.BlockSpec((1, N_MOD, D_MODEL), lambda b, t: (b, 0, 0)),
            pl.BlockSpec((1, D_MODEL), const2),
            pl.BlockSpec((D_MODEL, Z_COLS), const2),
            pl.BlockSpec((1, Q_LORA), const2),
            pl.BlockSpec((HEADS * Q_COLS, Q_LORA), const2),
            pl.BlockSpec((1, KV_LORA), const2),
            pl.BlockSpec((KV_LORA, HEADS * NOPE), const2),
            pl.BlockSpec((HEADS, V_DIM, KV_LORA), const3),
            pl.BlockSpec((FOURIER, 2 * FOURIER), const2),
            pl.BlockSpec((TOK_TILE, 2 * ROPE), lambda b, t: (t, 0)),
            pl.BlockSpec((ROPE, TOK_TILE), lambda b, t: (0, t)),
            pl.BlockSpec((ROPE, TOK_TILE), lambda b, t: (0, t)),
        ],
        out_specs=[
            pl.BlockSpec((1, HEADS, HEAD_PAD, TOK_TILE), lambda b, t: (b, 0, 0, t)),
            pl.BlockSpec((1, HEADS, TOK_TILE, HEAD_PAD), lambda b, t: (b, 0, t, 0)),
            pl.BlockSpec((1, HEADS, TOK_TILE // KV_TILE, V_ROWS, KV_TILE), lambda b, t: (b, 0, t, 0, 0)),
            pl.BlockSpec((1, rows_in, FOURIER), lambda b, t: (b, t, 0)),
            pl.BlockSpec((1, rows_in, FOURIER), lambda b, t: (b, t, 0)),
        ],
        out_shape=[
            jax.ShapeDtypeStruct((nb, HEADS, HEAD_PAD, SEQ), BF16),
            jax.ShapeDtypeStruct((nb, HEADS, SEQ, HEAD_PAD), BF16),
            jax.ShapeDtypeStruct((nb, HEADS, SEQ // KV_TILE, V_ROWS, KV_TILE), BF16),
            jax.ShapeDtypeStruct((nb, FFT_N1 * PITCH_IN, FOURIER), F32),
            jax.ShapeDtypeStruct((nb, FFT_N1 * PITCH_IN, FOURIER), F32),
        ],
        compiler_params=pltpu.CompilerParams(
            dimension_semantics=("parallel", "parallel"), vmem_limit_bytes=VMEM_LIMIT),
        name="pre",
    )(x, mod, gmix, win, gq, wuqt, gkv, wk, wvt, mcs, cs_t, cos_tr, sin_tr)


def _attn_kernel(qt_ref, k_ref, vt_ref, o_ref, *bufs):
    st_refs, p_refs = bufs[:SCORE_BUFS], bufs[SCORE_BUFS:]
    n_kv = SEQ // KV_TILE
    n_items = Q_SUBTILES * n_kv

    def scores(i):
        s, c = divmod(i, n_kv)
        kt = k_ref[0, 0, c * KV_TILE:(c + 1) * KV_TILE, :]
        qt = qt_ref[0, 0, :, s * Q_SUB:(s + 1) * Q_SUB]
        st = jnp.dot(kt, qt, preferred_element_type=F32)
        st_refs[i % SCORE_BUFS][...] = st
        return jnp.max(st, axis=0, keepdims=True)

    def value_matmul(i):
        return jnp.dot(vt_ref[0, 0, i % n_kv], p_refs[i % PROB_BUFS][...],
                       preferred_element_type=F32)

    chunk_max = [scores(i) for i in range(SCORE_BUFS - 1)]
    for i in range(n_items):
        s, c = divmod(i, n_kv)
        if i + SCORE_BUFS - 1 < n_items:
            chunk_max.append(scores(i + SCORE_BUFS - 1))
        m_new = chunk_max[i] if c == 0 else jnp.maximum(m, chunk_max[i])
        p_refs[i % PROB_BUFS][...] = jnp.exp2(st_refs[i % SCORE_BUFS][...] - m_new).astype(BF16)
        pv = value_matmul(i)
        acc = pv if c == 0 else jnp.exp2(m - m_new) * acc + pv
        m = m_new
        if c == n_kv - 1:
            o_ref[0, s * Q_SUB:(s + 1) * Q_SUB, :] = (acc[:V_DIM] / acc[V_DIM:V_DIM + 1]).T.astype(BF16)


def _attention(qt, k, vt):
    nb = qt.shape[0]
    return pl.pallas_call(
        _attn_kernel,
        grid=(nb, HEADS, SEQ // Q_TILE),
        in_specs=[
            pl.BlockSpec((1, 1, HEAD_PAD, Q_TILE), lambda b, h, i: (b, h, 0, i)),
            pl.BlockSpec((1, 1, SEQ, HEAD_PAD), lambda b, h, i: (b, h, 0, 0)),
            pl.BlockSpec((1, 1, SEQ // KV_TILE, V_ROWS, KV_TILE), lambda b, h, i: (b, h, 0, 0, 0)),
        ],
        out_specs=pl.BlockSpec((1, Q_TILE, V_DIM), lambda b, h, i: (b, i, h)),
        out_shape=jax.ShapeDtypeStruct((nb, SEQ, MLA_WIDTH), BF16),
        scratch_shapes=([pltpu.VMEM((KV_TILE, Q_SUB), F32)] * SCORE_BUFS
                        + [pltpu.VMEM((KV_TILE, Q_SUB), BF16)] * PROB_BUFS),
        compiler_params=pltpu.CompilerParams(
            dimension_semantics=("parallel", "parallel", "parallel"), vmem_limit_bytes=VMEM_LIMIT),
        name="attn",
    )(qt, k, vt)


def _seqdft_kernel(a_ref, b_ref, g1_ref, f2_ref, y_ref, t_ref):
    def stage1(n2, carry):
        xa = a_ref[0, pl.ds(n2, FFT_N1, stride=PITCH_IN), :]
        xb = b_ref[0, pl.ds(n2, FFT_N1, stride=PITCH_IN), :]
        xab = jnp.concatenate([xa, xb], axis=0).astype(BF16)
        t = jnp.dot(g1_ref[n2], xab, preferred_element_type=F32)
        t_ref[pl.ds(pl.multiple_of(n2 * PITCH_T, 8), 2 * FFT_N1), :] = t
        return carry

    lax.fori_loop(0, FFT_N2, stage1, 0, unroll=FFT_UNROLL)

    def stage2(k1, carry):
        tr = t_ref[pl.ds(k1, FFT_N2, stride=PITCH_T), :]
        ti = t_ref[pl.ds(FFT_N1 + k1, FFT_N2, stride=PITCH_T), :]
        tt = jnp.concatenate([tr, ti], axis=0).astype(BF16)
        y = jnp.dot(f2_ref[...], tt, preferred_element_type=F32)
        y_ref[0, pl.ds(k1, FFT_N2, stride=PITCH_OUT), :] = y
        return carry

    lax.fori_loop(0, FFT_N1, stage2, 0, unroll=FFT_UNROLL)

    zero = jnp.zeros((FFT_N2, LANES), F32)
    for j in range(PITCH_OUT - FFT_N1):
        y_ref[0, pl.ds(FFT_N1 + j, FFT_N2, stride=PITCH_OUT), :] = zero


def _seqdft(a, b, g1, f2):
    nb = a.shape[0]
    return pl.pallas_call(
        _seqdft_kernel,
        grid=(nb, FOURIER // LANES),
        in_specs=[
            pl.BlockSpec((1, FFT_N1 * PITCH_IN, LANES), lambda b, c: (b, 0, c)),
            pl.BlockSpec((1, FFT_N1 * PITCH_IN, LANES), lambda b, c: (b, 0, c)),
            pl.BlockSpec((FFT_N2, 2 * FFT_N1, 2 * FFT_N1), lambda b, c: (0, 0, 0)),
            pl.BlockSpec((FFT_N2, 2 * FFT_N2), lambda b, c: (0, 0)),
        ],
        out_specs=pl.BlockSpec((1, FFT_N2 * PITCH_OUT, LANES), lambda b, c: (b, 0, c)),
        out_shape=jax.ShapeDtypeStruct((nb, FFT_N2 * PITCH_OUT, FOURIER), F32),
        scratch_shapes=[pltpu.VMEM((FFT_N2 * PITCH_T, LANES), F32)],
        compiler_params=pltpu.CompilerParams(
            dimension_semantics=("parallel", "parallel"), vmem_limit_bytes=VMEM_LIMIT),
        name="seqdft",
    )(a, b, g1, f2)


MXU_TILE = 256
FF_CHUNK_COLS = 3 * MXU_TILE
FF_CHUNKS = tuple((lo, min(lo + FF_CHUNK_COLS, D_FF)) for lo in range(0, D_FF, FF_CHUNK_COLS))


def _post_kernel(x_ref, mod_ref, ym_ref, yf_ref, woa_ref, wob_ref, gffn_ref, wg_ref, wu_ref, wd_ref,
                 gfin_ref, o_ref):
    ga1 = mod_ref[0, 2:3, :]
    sh2 = mod_ref[0, 3:4, :]
    sc2 = mod_ref[0, 4:5, :]
    ga2 = mod_ref[0, 5:6, :]
    x = x_ref[0]
    yf = jnp.concatenate(
        [yf_ref[0, j * PITCH_OUT:j * PITCH_OUT + FFT_N1, :] for j in range(POST_TILE // FFT_N1)], axis=0)
    mix = (jnp.dot(ym_ref[0], woa_ref[...], preferred_element_type=F32)
           + jnp.dot(yf.astype(BF16), wob_ref[...], preferred_element_type=F32))
    x1 = x + ga1 * mix
    h2 = (_rms(x1, gffn_ref[...]) * (1.0 + sc2) + sh2).astype(BF16)
    ffn = None
    for lo, hi in FF_CHUNKS:
        g = jnp.dot(h2, wg_ref[:, lo:hi], preferred_element_type=F32)
        u = jnp.dot(h2, wu_ref[:, lo:hi], preferred_element_type=F32)
        act = (g * jax.nn.sigmoid(g) * u).astype(BF16)
        part = jnp.dot(act, wd_ref[lo:hi, :], preferred_element_type=F32)
        ffn = part if ffn is None else ffn + part
    x2 = x1 + ga2 * ffn
    o_ref[0] = _rms(x2, gfin_ref[...])


def _post(x, mod, ymla, yfour, woa, wob, gffn, wg, wu, wd, gfin):
    nb = x.shape[0]
    nt = SEQ // POST_TILE
    rows_out = POST_TILE // FFT_N1 * PITCH_OUT
    const2 = lambda b, t: (0, 0)
    once = pl.Buffered(1)
    return pl.pallas_call(
        _post_kernel,
        grid=(nb, nt),
        in_specs=[
            pl.BlockSpec((1, POST_TILE, D_MODEL), lambda b, t: (b, t, 0)),
            pl.BlockSpec((1, N_MOD, D_MODEL), lambda b, t: (b, 0, 0)),
            pl.BlockSpec((1, POST_TILE, MLA_WIDTH), lambda b, t: (b, t, 0)),
            pl.BlockSpec((1, rows_out, FOURIER), lambda b, t: (b, t, 0)),
            pl.BlockSpec((MLA_WIDTH, D_MODEL), const2, pipeline_mode=once),
            pl.BlockSpec((FOURIER, D_MODEL), const2, pipeline_mode=once),
            pl.BlockSpec((1, D_MODEL), const2),
            pl.BlockSpec((D_MODEL, D_FF), const2, pipeline_mode=once),
            pl.BlockSpec((D_MODEL, D_FF), const2, pipeline_mode=once),
            pl.BlockSpec((D_FF, D_MODEL), const2, pipeline_mode=once),
            pl.BlockSpec((1, D_MODEL), const2),
        ],
        out_specs=pl.BlockSpec((1, POST_TILE, D_MODEL), lambda b, t: (b, t, 0)),
        out_shape=jax.ShapeDtypeStruct((nb, SEQ, D_MODEL), F32),
        compiler_params=pltpu.CompilerParams(
            dimension_semantics=("parallel", "parallel"), vmem_limit_bytes=VMEM_LIMIT),
        name="post",
    )(x, mod, ymla, yfour, woa, wob, gffn, wg, wu, wd, gfin)


def _dft_constants():
    n = FFT_N1 * FFT_N2
    c = np.arange(F_GROUP_DIM)
    ang = 2.0 * np.pi * np.outer(c, c) / F_GROUP_DIM
    eye = np.eye(F_GROUPS)
    cbd = np.kron(eye, np.cos(ang) / math.sqrt(F_GROUP_DIM))
    sbd = np.kron(eye, np.sin(ang) / math.sqrt(F_GROUP_DIM))
    k1 = np.arange(FFT_N1)[None, :, None]
    n1 = np.arange(FFT_N1)[None, None, :]
    n2 = np.arange(FFT_N2)[:, None, None]
    phi = 2.0 * np.pi * ((k1 * (FFT_N2 * n1 + n2)) % n) / n
    cp, sp = np.cos(phi), np.sin(phi)
    g1 = np.concatenate([np.concatenate([cp, -sp], axis=2),
                         np.concatenate([-sp, -cp], axis=2)], axis=1)
    k2 = np.arange(FFT_N2)[:, None]
    m2 = np.arange(FFT_N2)[None, :]
    th = 2.0 * np.pi * ((k2 * m2) % FFT_N2) / FFT_N2
    f2 = np.concatenate([np.cos(th), np.sin(th)], axis=1) / math.sqrt(n)
    as32 = lambda a: np.asarray(a, np.float32)
    return as32(cbd), as32(sbd), as32(g1), as32(f2)


def _rope_tables():
    pos = jnp.arange(SEQ, dtype=F32)
    inv = 1.0 / (ROPE_THETA ** (jnp.arange(0, ROPE, 2, dtype=F32) / ROPE))
    ang = pos[:, None] * inv[None, :]
    cos, sin = jnp.cos(ang), jnp.sin(ang)
    return jnp.concatenate([cos, cos], axis=1), jnp.concatenate([-sin, sin], axis=1)


def _swap_halves(w):
    half = ROPE // 2
    return jnp.concatenate([w[:, half:], w[:, :half]], axis=1)


def kernel(x_prompt, x_sample, c_prompt, c_sample, w_ada, b_ada, g_mix, w_in, g_q_lat, w_uq, g_kv_lat,
           w_ukv, w_four, w_out, g_ffn, w_gate, w_up, w_down, g_final):
    l = 0
    cbd, sbd, g1, f2 = _dft_constants()
    g1 = jnp.asarray(g1).astype(BF16)
    f2 = jnp.asarray(f2).astype(BF16)
    cos2, sin2 = _rope_tables()
    cs_t = jnp.concatenate([cos2, sin2], axis=1)
    cos_tr, sin_tr = cos2.T, sin2.T

    nbp, nbs = x_prompt.shape[0], x_sample.shape[0]
    rows = -(-(nbp + nbs) // 8) * 8
    c_all = jnp.concatenate([c_prompt, c_sample, jnp.zeros((rows - nbp - nbs, D_MODEL), F32)], axis=0)
    mod = _modulation(c_all, w_ada[l].astype(BF16), b_ada[l][None, :])
    mod = mod.reshape(rows, N_MOD, D_MODEL)

    wbd = (jnp.eye(F_GROUPS, dtype=F32)[:, None, :, None] * w_four[l][:, :, None, :]).reshape(FOURIER, FOURIER)
    mcs = _fold_fourier(jnp.asarray(cbd), jnp.asarray(sbd), wbd)

    o1 = Q_LORA + KV_LORA
    o2 = o1 + ROPE
    wi = w_in[l]
    win = jnp.concatenate([wi[:, :o2], _swap_halves(wi[:, o1:o2]), wi[:, o2:]], axis=1).astype(BF16)
    wq = w_uq[l]
    q_parts = []
    for hd in range(HEADS):
        base = hd * QK_DIM
        pe = wq[:, base + NOPE:base + QK_DIM]
        q_parts += [wq[:, base:base + QK_DIM], _swap_halves(pe)]
    wuqt = jnp.concatenate(q_parts, axis=1).T.astype(BF16)
    wkv = w_ukv[l].reshape(KV_LORA, HEADS, NOPE + V_DIM)
    wk = wkv[:, :, :NOPE].reshape(KV_LORA, HEADS * NOPE).astype(BF16)
    wvt = jnp.transpose(wkv[:, :, NOPE:], (1, 2, 0)).astype(BF16)
    wo = w_out[l].astype(BF16)
    woa, wob = wo[:MLA_WIDTH], wo[MLA_WIDTH:]
    wg, wu, wd = w_gate[l].astype(BF16), w_up[l].astype(BF16), w_down[l].astype(BF16)
    gmix, gq, gkv = g_mix[l][None, :], g_q_lat[l][None, :], g_kv_lat[l][None, :]
    gffn, gfin = g_ffn[l][None, :], g_final[None, :]

    def trunk(x, mod_x):
        qt, k, vt, a, b = _pre(x, mod_x, gmix, win, gq, wuqt, gkv, wk, wvt, mcs, cs_t, cos_tr, sin_tr)
        ymla = _attention(qt, k, vt)
        yfour = _seqdft(a, b, g1, f2)
        return _post(x, mod_x, ymla, yfour, woa, wob, gffn, wg, wu, wd, gfin)

    y_prompt = trunk(x_prompt, mod[:nbp])
    y_sample = trunk(x_sample, mod[nbp:nbp + nbs])
    return (y_prompt, y_sample)
```

```python
import functools
import math

import numpy as np
import jax
import jax.numpy as jnp
from jax import lax
from jax.experimental import pallas as pl
from jax.experimental.pallas import tpu as pltpu

F32 = jnp.float32
BF16 = jnp.bfloat16

D_MODEL = 1024
SEQ = 8192
HEADS = 4
NOPE = 128
ROPE = 64
V_DIM = 128
QK_DIM = NOPE + ROPE
Q_LORA = 256
KV_LORA = 128
FOURIER = 512
F_GROUPS = 8
F_GROUP_DIM = 64
D_FF = 2816
N_MOD = 6
EPS = 1e-6
ROPE_THETA = 10000.0
MLA_WIDTH = HEADS * V_DIM

FFT_N1 = 64
FFT_N2 = 128
PITCH_IN = FFT_N2 + 8
PITCH_T = 2 * FFT_N1 + 8
PITCH_OUT = FFT_N1 + 8
LANES = 128
FFT_UNROLL = 64

HEAD_PAD = 256
Q_COLS = NOPE + 2 * ROPE
Z_COLS = Q_LORA + KV_LORA + 2 * ROPE + FOURIER

TOK_TILE = 1024
POST_TILE = 1024
Q_SUB = 256
Q_SUBTILES = 8
Q_TILE = Q_SUB * Q_SUBTILES
KV_TILE = 512
SCORE_BUFS = 4
PROB_BUFS = 3
V_ROWS = V_DIM + 16
VMEM_LIMIT = 56 * 1024 * 1024


def _rms(x, g):
    return x * lax.rsqrt(jnp.mean(x * x, axis=-1, keepdims=True) + EPS) * g


def _mod_kernel(c_ref, w_ref, b_ref, o_ref):
    c = c_ref[...]
    a = (c * jax.nn.sigmoid(c)).astype(BF16)
    o_ref[...] = jnp.dot(a, w_ref[...], preferred_element_type=F32) + b_ref[...]


def _modulation(c_all, w_ada, b_ada):
    rows = c_all.shape[0]
    blk = 1024
    return pl.pallas_call(
        _mod_kernel,
        grid=(N_MOD * D_MODEL // blk,),
        in_specs=[
            pl.BlockSpec((rows, D_MODEL), lambda j: (0, 0)),
            pl.BlockSpec((D_MODEL, blk), lambda j: (0, j)),
            pl.BlockSpec((1, blk), lambda j: (0, j)),
        ],
        out_specs=pl.BlockSpec((rows, blk), lambda j: (0, j)),
        out_shape=jax.ShapeDtypeStruct((rows, N_MOD * D_MODEL), F32),
        name="mod",
    )(c_all, w_ada, b_ada)


def _fold_kernel(c_ref, s_ref, w_ref, o_ref):
    w = w_ref[...]
    mc = jnp.dot(c_ref[...], w, preferred_element_type=F32, precision=lax.Precision.HIGHEST)
    ms = jnp.dot(s_ref[...], w, preferred_element_type=F32, precision=lax.Precision.HIGHEST)
    o_ref[:, :FOURIER] = mc.astype(BF16)
    o_ref[:, FOURIER:] = ms.astype(BF16)


def _fold_fourier(cbd, sbd, wbd):
    return pl.pallas_call(
        _fold_kernel,
        out_shape=jax.ShapeDtypeStruct((FOURIER, 2 * FOURIER), BF16),
        name="fold",
    )(cbd, sbd, wbd)


def _pre_kernel(x_ref, mod_ref, gmix_ref, win_ref, gq_ref, wuqt_ref, gkv_ref, wk_ref, wvt_ref,
                mcs_ref, cs_ref, cost_ref, sint_ref, qt_ref, k_ref, vt_ref, a_ref, b_ref):
    x = x_ref[0]
    sh1 = mod_ref[0, 0:1, :]
    sc1 = mod_ref[0, 1:2, :]
    h = _rms(x, gmix_ref[...]) * (1.0 + sc1) + sh1
    z = jnp.dot(h.astype(BF16), win_ref[...], preferred_element_type=F32)

    o_kv = Q_LORA
    o_kpe = o_kv + KV_LORA
    o_f = o_kpe + 2 * ROPE

    qn = _rms(z[:, :Q_LORA], gq_ref[...]).astype(BF16)
    qqt = lax.dot_general(wuqt_ref[...], qn, (((1,), (1,)), ((), ())),
                          preferred_element_type=F32)
    cos_tr = cost_ref[...]
    sin_tr = sint_ref[...]
    scale = math.log2(math.e) / math.sqrt(QK_DIM)
    for hd in range(HEADS):
        base = hd * Q_COLS
        rot = (qqt[base + NOPE:base + QK_DIM, :] * cos_tr
               + qqt[base + QK_DIM:base + Q_COLS, :] * sin_tr)
        qt_ref[0, hd, :NOPE, :] = (qqt[base:base + NOPE, :] * scale).astype(BF16)
        qt_ref[0, hd, NOPE:QK_DIM, :] = (rot * scale).astype(BF16)
        qt_ref[0, hd, QK_DIM:, :] = jnp.zeros((HEAD_PAD - QK_DIM, TOK_TILE), BF16)

    kvn = _rms(z[:, o_kv:o_kpe], gkv_ref[...]).astype(BF16)
    kn = jnp.dot(kvn, wk_ref[...], preferred_element_type=F32)
    t = z[:, o_kpe:o_f] * cs_ref[...]
    lane = lax.broadcasted_iota(jnp.int32, t.shape, 1)
    krot = jnp.where(lane < ROPE, t + pltpu.roll(t, ROPE, axis=1), 0.0)
    for hd in range(HEADS):
        kh = jnp.concatenate([kn[:, hd * NOPE:(hd + 1) * NOPE], krot], axis=-1)
        k_ref[0, hd] = kh.astype(BF16)
        vt = lax.dot_general(wvt_ref[hd], kvn, (((1,), (1,)), ((), ())),
                             preferred_element_type=F32)
        for c in range(TOK_TILE // KV_TILE):
            vt_ref[0, hd, c, :V_DIM, :] = vt[:, c * KV_TILE:(c + 1) * KV_TILE].astype(BF16)
            vt_ref[0, hd, c, V_DIM:, :] = jnp.ones((V_ROWS - V_DIM, KV_TILE), BF16)

    f = z[:, o_f:].astype(BF16)
    ab = jnp.dot(f, mcs_ref[...], preferred_element_type=F32)
    pad = jnp.zeros((PITCH_IN - FFT_N2, FOURIER), F32)
    for j in range(TOK_TILE // FFT_N2):
        r0 = j * PITCH_IN
        a_ref[0, r0:r0 + FFT_N2, :] = ab[j * FFT_N2:(j + 1) * FFT_N2, :FOURIER]
        a_ref[0, r0 + FFT_N2:r0 + PITCH_IN, :] = pad
        b_ref[0, r0:r0 + FFT_N2, :] = ab[j * FFT_N2:(j + 1) * FFT_N2, FOURIER:]
        b_ref[0, r0 + FFT_N2:r0 + PITCH_IN, :] = pad


def _pre(x, mod, gmix, win, gq, wuqt, gkv, wk, wvt, mcs, cs_t, cos_tr, sin_tr):
    nb = x.shape[0]
    nt = SEQ // TOK_TILE
    rows_in = TOK_TILE // FFT_N2 * PITCH_IN
    const2 = lambda b, t: (0, 0)
    const3 = lambda b, t: (0, 0, 0)
    return pl.pallas_call(
        _pre_kernel,
        grid=(nb, nt),
        in_specs=[
            pl.BlockSpec((1, TOK_TILE, D_MODEL), lambda b, t: (b, t, 0)),
            pl.BlockSpec((1, N_MOD, D_MODEL), lambda b, t: (b, 0, 0)),
            pl.BlockSpec((1, D_MODEL), const2),
            pl.BlockSpec((D_MODEL, Z_COLS), const2),
            pl.BlockSpec((1, Q_LORA), const2),
            pl.BlockSpec((HEADS * Q_COLS, Q_LORA), const2),
            pl.BlockSpec((1, KV_LORA), const2),
            pl.BlockSpec((KV_LORA, HEADS * NOPE), const2),
            pl.BlockSpec((HEADS, V_DIM, KV_LORA), const3),
            pl.BlockSpec((FOURIER, 2 * FOURIER), const2),
            pl.BlockSpec((TOK_TILE, 2 * ROPE), lambda b, t: (t, 0)),
            pl.BlockSpec((ROPE, TOK_TILE), lambda b, t: (0, t)),
            pl.BlockSpec((ROPE, TOK_TILE), lambda b, t: (0, t)),
        ],
        out_specs=[
            pl.BlockSpec((1, HEADS, HEAD_PAD, TOK_TILE), lambda b, t: (b, 0, 0, t)),
            pl.BlockSpec((1, HEADS, TOK_TILE, HEAD_PAD), lambda b, t: (b, 0, t, 0)),
            pl.BlockSpec((1, HEADS, TOK_TILE // KV_TILE, V_ROWS, KV_TILE), lambda b, t: (b, 0, t, 0, 0)),
            pl.BlockSpec((1, rows_in, FOURIER), lambda b, t: (b, t, 0)),
            pl.BlockSpec((1, rows_in, FOURIER), lambda b, t: (b, t, 0)),
        ],
        out_shape=[
            jax.ShapeDtypeStruct((nb, HEADS, HEAD_PAD, SEQ), BF16),
            jax.ShapeDtypeStruct((nb, HEADS, SEQ, HEAD_PAD), BF16),
            jax.ShapeDtypeStruct((nb, HEADS, SEQ // KV_TILE, V_ROWS, KV_TILE), BF16),
            jax.ShapeDtypeStruct((nb, FFT_N1 * PITCH_IN, FOURIER), F32),
            jax.ShapeDtypeStruct((nb, FFT_N1 * PITCH_IN, FOURIER), F32),
        ],
        compiler_params=pltpu.CompilerParams(
            dimension_semantics=("parallel", "parallel"), vmem_limit_bytes=VMEM_LIMIT),
        name="pre",
    )(x, mod, gmix, win, gq, wuqt, gkv, wk, wvt, mcs, cs_t, cos_tr, sin_tr)


def _attn_kernel(qt_ref, k_ref, vt_ref, o_ref, *bufs):
    st_refs, p_refs = bufs[:SCORE_BUFS], bufs[SCORE_BUFS:]
    n_kv = SEQ // KV_TILE
    n_items = Q_SUBTILES * n_kv

    def scores(i):
        s, c = divmod(i, n_kv)
        kt = k_ref[0, 0, c * KV_TILE:(c + 1) * KV_TILE, :]
        qt = qt_ref[0, 0, :, s * Q_SUB:(s + 1) * Q_SUB]
        st = jnp.dot(kt, qt, preferred_element_type=F32)
        st_refs[i % SCORE_BUFS][...] = st
        return jnp.max(st, axis=0, keepdims=True)

    def value_matmul(i):
        return jnp.dot(vt_ref[0, 0, i % n_kv], p_refs[i % PROB_BUFS][...],
                       preferred_element_type=F32)

    chunk_max = [scores(i) for i in range(SCORE_BUFS - 1)]
    for i in range(n_items):
        s, c = divmod(i, n_kv)
        if i + SCORE_BUFS - 1 < n_items:
            chunk_max.append(scores(i + SCORE_BUFS - 1))
        m_new = chunk_max[i] if c == 0 else jnp.maximum(m, chunk_max[i])
        p_refs[i % PROB_BUFS][...] = jnp.exp2(st_refs[i % SCORE_BUFS][...] - m_new).astype(BF16)
        pv = value_matmul(i)
        acc = pv if c == 0 else jnp.exp2(m - m_new) * acc + pv
        m = m_new
        if c == n_kv - 1:
            o_ref[0, s * Q_SUB:(s + 1) * Q_SUB, :] = (acc[:V_DIM] / acc[V_DIM:V_DIM + 1]).T.astype(BF16)


def _attention(qt, k, vt):
    nb = qt.shape[0]
    return pl.pallas_call(
        _attn_kernel,
        grid=(nb, HEADS, SEQ // Q_TILE),
        in_specs=[
            pl.BlockSpec((1, 1, HEAD_PAD, Q_TILE), lambda b, h, i: (b, h, 0, i)),
            pl.BlockSpec((1, 1, SEQ, HEAD_PAD), lambda b, h, i: (b, h, 0, 0)),
            pl.BlockSpec((1, 1, SEQ // KV_TILE, V_ROWS, KV_TILE), lambda b, h, i: (b, h, 0, 0, 0)),
        ],
        out_specs=pl.BlockSpec((1, Q_TILE, V_DIM), lambda b, h, i: (b, i, h)),
        out_shape=jax.ShapeDtypeStruct((nb, SEQ, MLA_WIDTH), BF16),
        scratch_shapes=([pltpu.VMEM((KV_TILE, Q_SUB), F32)] * SCORE_BUFS
                        + [pltpu.VMEM((KV_TILE, Q_SUB), BF16)] * PROB_BUFS),
        compiler_params=pltpu.CompilerParams(
            dimension_semantics=("parallel", "parallel", "parallel"), vmem_limit_bytes=VMEM_LIMIT),
        name="attn",
    )(qt, k, vt)


def _seqdft_kernel(a_ref, b_ref, g1_ref, f2_ref, y_ref, t_ref):
    half_n2, half_n1 = FFT_N2 // 2, FFT_N1 // 2
    zeros = jnp.zeros((2 * FFT_N1, LANES), BF16)

    def column_block(n2):
        xa = a_ref[0, pl.ds(n2, FFT_N1, stride=PITCH_IN), :]
        xb = b_ref[0, pl.ds(n2, FFT_N1, stride=PITCH_IN), :]
        return jnp.concatenate([xa, xb], axis=0).astype(BF16)

    def stage1(i, carry):
        w = jnp.concatenate([jnp.concatenate([column_block(i), zeros], axis=1),
                             jnp.concatenate([zeros, column_block(i + half_n2)], axis=1)], axis=0)
        t = jnp.dot(g1_ref[i], w, preferred_element_type=F32)
        t_ref[pl.ds(pl.multiple_of(i * PITCH_T, 8), 2 * FFT_N1), :] = t[:, :LANES]
        t_ref[pl.ds(pl.multiple_of((i + half_n2) * PITCH_T, 8), 2 * FFT_N1), :] = t[:, LANES:]
        return carry

    lax.fori_loop(0, half_n2, stage1, 0, unroll=FFT_UNROLL // 2)

    def row_block(k1):
        tr = t_ref[pl.ds(k1, FFT_N2, stride=PITCH_T), :]
        ti = t_ref[pl.ds(FFT_N1 + k1, FFT_N2, stride=PITCH_T), :]
        return jnp.concatenate([tr, ti], axis=0).astype(BF16)

    def stage2(i, carry):
        tt = jnp.concatenate([row_block(i), row_block(i + half_n1)], axis=1)
        y = jnp.dot(f2_ref[...], tt, preferred_element_type=F32)
        y_ref[0, pl.ds(i, FFT_N2, stride=PITCH_OUT), :] = y[:, :LANES]
        y_ref[0, pl.ds(i + half_n1, FFT_N2, stride=PITCH_OUT), :] = y[:, LANES:]
        return carry

    lax.fori_loop(0, half_n1, stage2, 0, unroll=FFT_UNROLL // 2)

    zero = jnp.zeros((FFT_N2, LANES), F32)
    for j in range(PITCH_OUT - FFT_N1):
        y_ref[0, pl.ds(FFT_N1 + j, FFT_N2, stride=PITCH_OUT), :] = zero


def _seqdft(a, b, g1, f2):
    nb = a.shape[0]
    return pl.pallas_call(
        _seqdft_kernel,
        grid=(nb, FOURIER // LANES),
        in_specs=[
            pl.BlockSpec((1, FFT_N1 * PITCH_IN, LANES), lambda b, c: (b, 0, c)),
            pl.BlockSpec((1, FFT_N1 * PITCH_IN, LANES), lambda b, c: (b, 0, c)),
            pl.BlockSpec((FFT_N2 // 2, 2 * FFT_N1, 4 * FFT_N1), lambda b, c: (0, 0, 0)),
            pl.BlockSpec((FFT_N2, 2 * FFT_N2), lambda b, c: (0, 0)),
        ],
        out_specs=pl.BlockSpec((1, FFT_N2 * PITCH_OUT, LANES), lambda b, c: (b, 0, c)),
        out_shape=jax.ShapeDtypeStruct((nb, FFT_N2 * PITCH_OUT, FOURIER), F32),
        scratch_shapes=[pltpu.VMEM((FFT_N2 * PITCH_T, LANES), F32)],
        compiler_params=pltpu.CompilerParams(
            dimension_semantics=("parallel", "parallel"), vmem_limit_bytes=VMEM_LIMIT),
        name="seqdft",
    )(a, b, g1, f2)


MXU_TILE = 256
FF_CHUNK_COLS = 3 * MXU_TILE
FF_CHUNKS = tuple((lo, min(lo + FF_CHUNK_COLS, D_FF)) for lo in range(0, D_FF, FF_CHUNK_COLS))


def _post_kernel(x_ref, mod_ref, ym_ref, yf_ref, woa_ref, wob_ref, gffn_ref, wg_ref, wu_ref, wd_ref,
                 gfin_ref, o_ref):
    ga1 = mod_ref[0, 2:3, :]
    sh2 = mod_ref[0, 3:4, :]
    sc2 = mod_ref[0, 4:5, :]
    ga2 = mod_ref[0, 5:6, :]
    x = x_ref[0]
    yf = jnp.concatenate(
        [yf_ref[0, j * PITCH_OUT:j * PITCH_OUT + FFT_N1, :] for j in range(POST_TILE // FFT_N1)], axis=0)
    mix = (jnp.dot(ym_ref[0], woa_ref[...], preferred_element_type=F32)
           + jnp.dot(yf.astype(BF16), wob_ref[...], preferred_element_type=F32))
    x1 = x + ga1 * mix
    h2 = (_rms(x1, gffn_ref[...]) * (1.0 + sc2) + sh2).astype(BF16)
    ffn = None
    for lo, hi in FF_CHUNKS:
        g = jnp.dot(h2, wg_ref[:, lo:hi], preferred_element_type=F32)
        u = jnp.dot(h2, wu_ref[:, lo:hi], preferred_element_type=F32)
        act = (g * jax.nn.sigmoid(g) * u).astype(BF16)
        part = jnp.dot(act, wd_ref[lo:hi, :], preferred_element_type=F32)
        ffn = part if ffn is None else ffn + part
    x2 = x1 + ga2 * ffn
    o_ref[0] = _rms(x2, gfin_ref[...])


def _post(x, mod, ymla, yfour, woa, wob, gffn, wg, wu, wd, gfin):
    nb = x.shape[0]
    nt = SEQ // POST_TILE
    rows_out = POST_TILE // FFT_N1 * PITCH_OUT
    const2 = lambda b, t: (0, 0)
    once = pl.Buffered(1)
    return pl.pallas_call(
        _post_kernel,
        grid=(nb, nt),
        in_specs=[
            pl.BlockSpec((1, POST_TILE, D_MODEL), lambda b, t: (b, t, 0)),
            pl.BlockSpec((1, N_MOD, D_MODEL), lambda b, t: (b, 0, 0)),
            pl.BlockSpec((1, POST_TILE, MLA_WIDTH), lambda b, t: (b, t, 0)),
            pl.BlockSpec((1, rows_out, FOURIER), lambda b, t: (b, t, 0)),
            pl.BlockSpec((MLA_WIDTH, D_MODEL), const2, pipeline_mode=once),
            pl.BlockSpec((FOURIER, D_MODEL), const2, pipeline_mode=once),
            pl.BlockSpec((1, D_MODEL), const2),
            pl.BlockSpec((D_MODEL, D_FF), const2, pipeline_mode=once),
            pl.BlockSpec((D_MODEL, D_FF), const2, pipeline_mode=once),
            pl.BlockSpec((D_FF, D_MODEL), const2, pipeline_mode=once),
            pl.BlockSpec((1, D_MODEL), const2),
        ],
        out_specs=pl.BlockSpec((1, POST_TILE, D_MODEL), lambda b, t: (b, t, 0)),
        out_shape=jax.ShapeDtypeStruct((nb, SEQ, D_MODEL), F32),
        compiler_params=pltpu.CompilerParams(
            dimension_semantics=("parallel", "parallel"), vmem_limit_bytes=VMEM_LIMIT),
        name="post",
    )(x, mod, ymla, yfour, woa, wob, gffn, wg, wu, wd, gfin)


def _dft_constants():
    n = FFT_N1 * FFT_N2
    c = np.arange(F_GROUP_DIM)
    ang = 2.0 * np.pi * np.outer(c, c) / F_GROUP_DIM
    eye = np.eye(F_GROUPS)
    cbd = np.kron(eye, np.cos(ang) / math.sqrt(F_GROUP_DIM))
    sbd = np.kron(eye, np.sin(ang) / math.sqrt(F_GROUP_DIM))
    k1 = np.arange(FFT_N1)[None, :, None]
    n1 = np.arange(FFT_N1)[None, None, :]
    n2 = np.arange(FFT_N2)[:, None, None]
    phi = 2.0 * np.pi * ((k1 * (FFT_N2 * n1 + n2)) % n) / n
    cp, sp = np.cos(phi), np.sin(phi)
    g1 = np.concatenate([np.concatenate([cp, -sp], axis=2),
                         np.concatenate([-sp, -cp], axis=2)], axis=1)
    g1 = np.concatenate([g1[:FFT_N2 // 2], g1[FFT_N2 // 2:]], axis=2)
    k2 = np.arange(FFT_N2)[:, None]
    m2 = np.arange(FFT_N2)[None, :]
    th = 2.0 * np.pi * ((k2 * m2) % FFT_N2) / FFT_N2
    f2 = np.concatenate([np.cos(th), np.sin(th)], axis=1) / math.sqrt(n)
    as32 = lambda a: np.asarray(a, np.float32)
    return as32(cbd), as32(sbd), as32(g1), as32(f2)


def _rope_tables():
    pos = jnp.arange(SEQ, dtype=F32)
    inv = 1.0 / (ROPE_THETA ** (jnp.arange(0, ROPE, 2, dtype=F32) / ROPE))
    ang = pos[:, None] * inv[None, :]
    cos, sin = jnp.cos(ang), jnp.sin(ang)
    return jnp.concatenate([cos, cos], axis=1), jnp.concatenate([-sin, sin], axis=1)


def _swap_halves(w):
    half = ROPE // 2
    return jnp.concatenate([w[:, half:], w[:, :half]], axis=1)


def kernel(x_prompt, x_sample, c_prompt, c_sample, w_ada, b_ada, g_mix, w_in, g_q_lat, w_uq, g_kv_lat,
           w_ukv, w_four, w_out, g_ffn, w_gate, w_up, w_down, g_final):
    l = 0
    cbd, sbd, g1, f2 = _dft_constants()
    g1 = jnp.asarray(g1).astype(BF16)
    f2 = jnp.asarray(f2).astype(BF16)
    cos2, sin2 = _rope_tables()
    cs_t = jnp.concatenate([cos2, sin2], axis=1)
    cos_tr, sin_tr = cos2.T, sin2.T

    nbp, nbs = x_prompt.shape[0], x_sample.shape[0]
    rows = -(-(nbp + nbs) // 8) * 8
    c_all = jnp.concatenate([c_prompt, c_sample, jnp.zeros((rows - nbp - nbs, D_MODEL), F32)], axis=0)
    mod = _modulation(c_all, w_ada[l].astype(BF16), b_ada[l][None, :])
    mod = mod.reshape(rows, N_MOD, D_MODEL)

    wbd = (jnp.eye(F_GROUPS, dtype=F32)[:, None, :, None] * w_four[l][:, :, None, :]).reshape(FOURIER, FOURIER)
    mcs = _fold_fourier(jnp.asarray(cbd), jnp.asarray(sbd), wbd)

    o1 = Q_LORA + KV_LORA
    o2 = o1 + ROPE
    wi = w_in[l]
    win = jnp.concatenate([wi[:, :o2], _swap_halves(wi[:, o1:o2]), wi[:, o2:]], axis=1).astype(BF16)
    wq = w_uq[l]
    q_parts = []
    for hd in range(HEADS):
        base = hd * QK_DIM
        pe = wq[:, base + NOPE:base + QK_DIM]
        q_parts += [wq[:, base:base + QK_DIM], _swap_halves(pe)]
    wuqt = jnp.concatenate(q_parts, axis=1).T.astype(BF16)
    wkv = w_ukv[l].reshape(KV_LORA, HEADS, NOPE + V_DIM)
    wk = wkv[:, :, :NOPE].reshape(KV_LORA, HEADS * NOPE).astype(BF16)
    wvt = jnp.transpose(wkv[:, :, NOPE:], (1, 2, 0)).astype(BF16)
    wo = w_out[l].astype(BF16)
    woa, wob = wo[:MLA_WIDTH], wo[MLA_WIDTH:]
    wg, wu, wd = w_gate[l].astype(BF16), w_up[l].astype(BF16), w_down[l].astype(BF16)
    gmix, gq, gkv = g_mix[l][None, :], g_q_lat[l][None, :], g_kv_lat[l][None, :]
    gffn, gfin = g_ffn[l][None, :], g_final[None, :]

    def trunk(x, mod_x):
        qt, k, vt, a, b = _pre(x, mod_x, gmix, win, gq, wuqt, gkv, wk, wvt, mcs, cs_t, cos_tr, sin_tr)
        ymla = _attention(qt, k, vt)
        yfour = _seqdft(a, b, g1, f2)
        return _post(x, mod_x, ymla, yfour, woa, wob, gffn, wg, wu, wd, gfin)

    y_prompt = trunk(x_prompt, mod[:nbp])
    y_sample = trunk(x_sample, mod[nbp:nbp + nbs])
    return (y_prompt, y_sample)
```
